```python
import jax, jax.numpy as jnp
from jax import lax
import numpy as np

D_MODEL = 2048
BATCH = 1
SEQ = 8192
DEPTH = 1
DEC_BATCH = 8
DEC_SEQ = 2048
PAST_LEN = 128

GRID_W = 64
HEAD_DIM = 128
NA_HEADS = 8
NA_WIDTH = NA_HEADS * HEAD_DIM
NA_KH_MAX = 8
NA_KW = 16
GQA_HEADS = 8
GQA_KV_HEADS = 2
GQA_WIDTH = GQA_HEADS * HEAD_DIM
GQA_KV_WIDTH = GQA_KV_HEADS * HEAD_DIM
ROPE_THETA = 10000.0
ROPE_AXIS_DIM = HEAD_DIM // 2
Q_BLOCK = 128
EPS = 1e-6
IN_SIZES = (NA_WIDTH, NA_WIDTH, NA_WIDTH, NA_WIDTH,
            GQA_WIDTH, GQA_KV_WIDTH, GQA_KV_WIDTH, GQA_WIDTH,
            D_MODEL, D_MODEL)
IN_WIDTH = 4 * NA_WIDTH + 2 * GQA_WIDTH + 2 * GQA_KV_WIDTH + 2 * D_MODEL

kernel_name = "hybrid_natten_gqa_gated_encoder"


def _split_points():
    pts = []
    acc = 0
    for s in IN_SIZES[:-1]:
        acc += s
        pts.append(acc)
    return pts


def rmsnorm(x, g):
    xf = x.astype(jnp.float32)
    y = xf * lax.rsqrt(jnp.mean(xf * xf, axis=-1, keepdims=True) + EPS)
    return (y * g.astype(jnp.float32)).astype(x.dtype)


def neighbourhood_attention(q, k, v, rpb):
    B, L, H, Dh = q.shape
    rows = L // GRID_W
    kh = min(NA_KH_MAX, rows)
    qg = q.reshape(B, rows, GRID_W, H, Dh)
    kg = k.reshape(B, rows, GRID_W, H, Dh)
    vg = v.reshape(B, rows, GRID_W, H, Dh)
    col = jnp.arange(GRID_W)
    col_start = jnp.clip(col - NA_KW // 2, 0, GRID_W - NA_KW)
    col_idx = col_start[:, None] + jnp.arange(NA_KW)[None, :]
    dc_idx = col_idx - col[:, None] + (NA_KW - 1)
    scale = Dh ** -0.5

    def one_row(r):
        rs = jnp.clip(r - kh // 2, 0, rows - kh)
        q_r = lax.dynamic_index_in_dim(qg, r, axis=1, keepdims=False)
        k_r = lax.dynamic_slice_in_dim(kg, rs, kh, axis=1)
        v_r = lax.dynamic_slice_in_dim(vg, rs, kh, axis=1)
        k_n = k_r[:, :, col_idx]
        v_n = v_r[:, :, col_idx]
        s = jnp.einsum('bqhd,bjqkhd->bhqjk', q_r, k_n,
                       preferred_element_type=jnp.float32) * scale
        dr_idx = rs + jnp.arange(kh) - r + (NA_KH_MAX - 1)
        bias = rpb[:, dr_idx][:, :, dc_idx]
        s = s + bias.transpose(0, 2, 1, 3).astype(jnp.float32)[None]
        p = jax.nn.softmax(s.reshape(B, H, GRID_W, kh * NA_KW), axis=-1)
        p = p.reshape(B, H, GRID_W, kh, NA_KW).astype(v.dtype)
        return jnp.einsum('bhqjk,bjqkhd->bqhd', p, v_n)

    out = lax.map(one_row, jnp.arange(rows))
    return out.transpose(1, 0, 2, 3, 4).reshape(B, L, H * Dh)


def axial_rope_tables(L):
    t = jnp.arange(L)
    r = (t // GRID_W).astype(jnp.float32)
    c = (t % GRID_W).astype(jnp.float32)
    freqs = ROPE_THETA ** (-jnp.arange(0, ROPE_AXIS_DIM, 2, dtype=jnp.float32) / ROPE_AXIS_DIM)
    ang_r = r[:, None] * freqs[None, :]
    ang_c = c[:, None] * freqs[None, :]
    return jnp.cos(ang_r), jnp.sin(ang_r), jnp.cos(ang_c), jnp.sin(ang_c)


def _rotate(x, cos, sin):
    half = x.shape[-1] // 2
    x1, x2 = x[..., :half], x[..., half:]
    c = cos[None, :, None, :]
    s = sin[None, :, None, :]
    return jnp.concatenate([x1 * c - x2 * s, x2 * c + x1 * s], axis=-1)


def apply_axial_rope(x, tables):
    cos_r, sin_r, cos_c, sin_c = tables
    xf = x.astype(jnp.float32)
    y = jnp.concatenate([_rotate(xf[..., :ROPE_AXIS_DIM], cos_r, sin_r),
                         _rotate(xf[..., ROPE_AXIS_DIM:], cos_c, sin_c)], axis=-1)
    return y.astype(x.dtype)


def gqa_attention(q, k, v):
    B, L, H, Dh = q.shape
    Hkv = k.shape[2]
    G = H // Hkv
    nb = L // Q_BLOCK
    scale = Dh ** -0.5
    qb = q.reshape(B, nb, Q_BLOCK, Hkv, G, Dh).transpose(1, 0, 2, 3, 4, 5)

    def block(q_blk):
        s = jnp.einsum('bqkgd,bskd->bkgqs', q_blk, k,
                       preferred_element_type=jnp.float32) * scale
        p = jax.nn.softmax(s, axis=-1).astype(v.dtype)
        return jnp.einsum('bkgqs,bskd->bqkgd', p, v)

    o = lax.map(block, qb)
    return o.transpose(1, 0, 2, 3, 4, 5).reshape(B, L, H * Dh)


def encoder_layer(x, norm_g, w_in, na_q_g, na_k_g, na_rpb, gq_q_g, gq_k_g,
                  w_branch_a, w_branch_b, gate_bias, w_out):
    B, L, _ = x.shape
    h = rmsnorm(x, norm_g)
    proj = h @ w_in
    (na_q, na_k, na_v, na_z, gq_q, gq_k, gq_v, gq_z,
     g_a, g_b) = jnp.split(proj, _split_points(), axis=-1)

    qa = rmsnorm(na_q.reshape(B, L, NA_HEADS, HEAD_DIM), na_q_g)
    ka = rmsnorm(na_k.reshape(B, L, NA_HEADS, HEAD_DIM), na_k_g)
    va = na_v.reshape(B, L, NA_HEADS, HEAD_DIM)
    oa = neighbourhood_attention(qa, ka, va, na_rpb) * jax.nn.silu(na_z)
    pa = oa @ w_branch_a

    tables = axial_rope_tables(L)
    qb = apply_axial_rope(rmsnorm(gq_q.reshape(B, L, GQA_HEADS, HEAD_DIM), gq_q_g), tables)
    kb = apply_axial_rope(rmsnorm(gq_k.reshape(B, L, GQA_KV_HEADS, HEAD_DIM), gq_k_g), tables)
    vb = gq_v.reshape(B, L, GQA_KV_HEADS, HEAD_DIM)
    ob = gqa_attention(qb, kb, vb) * jax.nn.silu(gq_z)
    pb = ob @ w_branch_b

    ga = jax.nn.sigmoid(g_a + gate_bias[:D_MODEL])
    gb = jax.nn.sigmoid(g_b + gate_bias[D_MODEL:])
    merged = ga * pa + gb * pb
    return x + merged @ w_out


def setup_inputs(seed: int = 0) -> dict:
    key = jax.random.key(seed)
    ks = jax.random.split(key, 16)
    f32 = jnp.float32
    nrm = lambda k, shape, s: (jax.random.normal(k, shape, f32) * s).astype(f32)
    return {
        "x_prompt": nrm(ks[0], (BATCH, SEQ, D_MODEL), 1.0),
        "x_sample": nrm(ks[1], (DEC_BATCH, DEC_SEQ, D_MODEL), 1.0),
        "norm_g": 1.0 + nrm(ks[2], (DEPTH, D_MODEL), 0.02),
        "w_in": nrm(ks[3], (DEPTH, D_MODEL, IN_WIDTH), D_MODEL ** -0.5),
        "na_q_g": 1.0 + nrm(ks[4], (DEPTH, HEAD_DIM), 0.02),
        "na_k_g": 1.0 + nrm(ks[5], (DEPTH, HEAD_DIM), 0.02),
        "na_rpb": nrm(ks[6], (DEPTH, NA_HEADS, 2 * NA_KH_MAX - 1, 2 * NA_KW - 1), 0.1),
        "gq_q_g": 1.0 + nrm(ks[7], (DEPTH, HEAD_DIM), 0.02),
        "gq_k_g": 1.0 + nrm(ks[8], (DEPTH, HEAD_DIM), 0.02),
        "w_branch_a": nrm(ks[9], (DEPTH, NA_WIDTH, D_MODEL), NA_WIDTH ** -0.5),
        "w_branch_b": nrm(ks[10], (DEPTH, GQA_WIDTH, D_MODEL), GQA_WIDTH ** -0.5),
        "gate_bias": nrm(ks[11], (DEPTH, 2 * D_MODEL), 0.02),
        "w_out": nrm(ks[12], (DEPTH, D_MODEL, D_MODEL), D_MODEL ** -0.5),
    }


def reference(x_prompt, x_sample, norm_g, w_in, na_q_g, na_k_g, na_rpb, gq_q_g, gq_k_g,
              w_branch_a, w_branch_b, gate_bias, w_out):
    y_prompt = x_prompt
    y_sample = x_sample
    for l in range(DEPTH):
        y_prompt = encoder_layer(y_prompt, norm_g[l], w_in[l], na_q_g[l], na_k_g[l], na_rpb[l],
                                 gq_q_g[l], gq_k_g[l], w_branch_a[l], w_branch_b[l],
                                 gate_bias[l], w_out[l])
        y_sample = encoder_layer(y_sample, norm_g[l], w_in[l], na_q_g[l], na_k_g[l], na_rpb[l],
                                 gq_q_g[l], gq_k_g[l], w_branch_a[l], w_branch_b[l],
                                 gate_bias[l], w_out[l])
    return (y_prompt, y_sample)
```

```python
import functools

import numpy as np
import jax
import jax.numpy as jnp
from jax import lax
from jax.experimental import pallas as pl
from jax.experimental.pallas import tpu as pltpu

F32 = jnp.float32
BF16 = jnp.bfloat16

D_MODEL = 2048
HEAD_DIM = 128
GRID_W = 64
NA_HEADS = 8
NA_KH = 8
NA_KW = 16
GQA_HEADS = 8
GQA_KV_HEADS = 2
GQA_GROUP = GQA_HEADS // GQA_KV_HEADS
ROPE_THETA = 10000.0
EPS = 1e-6
SCALE = HEAD_DIM ** -0.5

NA_W = NA_HEADS * HEAD_DIM
GQ_W = GQA_HEADS * HEAD_DIM
KV_W = GQA_KV_HEADS * HEAD_DIM
OFF_NA_Q = 0
OFF_NA_K = OFF_NA_Q + NA_W
OFF_NA_V = OFF_NA_K + NA_W
OFF_NA_Z = OFF_NA_V + NA_W
OFF_GQ_Q = OFF_NA_Z + NA_W
OFF_GQ_K = OFF_GQ_Q + GQ_W
OFF_GQ_V = OFF_GQ_K + KV_W
OFF_GQ_Z = OFF_GQ_V + KV_W
OFF_GA = OFF_GQ_Z + GQ_W
OFF_GB = OFF_GA + D_MODEL
IN_WIDTH = OFF_GB + D_MODEL


def _slab(off):
    return (off + 2 * D_MODEL) % IN_WIDTH

VMEM_LIMIT_BYTES = 56 * 1024 * 1024

PROJ_TM = 1024
PROJ_TN = 512
NA_QROWS = 4
NA_WROWS = NA_QROWS + NA_KH - 1
NA_TQ = NA_QROWS * GRID_W
NA_TK = NA_WROWS * GRID_W
GQA_TQ = 256
GQA_TK = 512
TAIL_TM = 256
NEG_BIG = -1e30


def _sigmoid(x):
    return 1.0 / (1.0 + jnp.exp(-x))


def _proj_kernel(x_ref, ng_ref, w_ref, hg_ref, gb_ref, cos_ref, sin_ref, o_ref, h_scr):
    j = pl.program_id(1)

    @pl.when(j == 0)
    def _():
        x = x_ref[...]
        ms = jnp.mean(x * x, axis=-1, keepdims=True)
        h_scr[...] = (x * lax.rsqrt(ms + EPS) * ng_ref[...]).astype(BF16)

    acc = jnp.dot(h_scr[...], w_ref[...], preferred_element_type=F32)
    nh = PROJ_TN // HEAD_DIM

    def head(a, hh):
        return a[:, hh * HEAD_DIM:(hh + 1) * HEAD_DIM]

    def put(hh, val):
        o_ref[:, hh * HEAD_DIM:(hh + 1) * HEAD_DIM] = val.astype(o_ref.dtype)

    def headnorm(xh, g):
        ms = jnp.mean(xh * xh, axis=-1, keepdims=True)
        return xh * lax.rsqrt(ms + EPS) * g

    def rope(y):
        lane = lax.broadcasted_iota(jnp.int32, y.shape, 1)
        first_half = (lane % (HEAD_DIM // 2)) < (HEAD_DIM // 4)
        partner = jnp.where(first_half,
                            pltpu.roll(y, HEAD_DIM - HEAD_DIM // 4, 1),
                            pltpu.roll(y, HEAD_DIM // 4, 1))
        return y * cos_ref[...] + partner * sin_ref[...]

    def tiles(off, width):
        return off // PROJ_TN, (off + width) // PROJ_TN

    def in_range(lo_hi):
        return jnp.logical_and(j >= lo_hi[0], j < lo_hi[1])

    @pl.when(in_range(tiles(OFF_NA_Q, NA_W)))
    def _():
        for hh in range(nh):
            put(hh, headnorm(head(acc, hh), hg_ref[0:1, :]))

    @pl.when(in_range(tiles(OFF_NA_K, NA_W)))
    def _():
        for hh in range(nh):
            put(hh, headnorm(head(acc, hh), hg_ref[1:2, :]))

    @pl.when(in_range(tiles(OFF_NA_V, NA_W)))
    def _():
        o_ref[...] = acc.astype(o_ref.dtype)

    @pl.when(jnp.logical_or(in_range(tiles(OFF_NA_Z, NA_W)), in_range(tiles(OFF_GQ_Z, GQ_W))))
    def _():
        o_ref[...] = (acc * _sigmoid(acc)).astype(o_ref.dtype)

    @pl.when(in_range(tiles(OFF_GQ_Q, GQ_W)))
    def _():
        for hh in range(nh):
            put(hh, rope(headnorm(head(acc, hh), hg_ref[2:3, :])))

    @pl.when(j == OFF_GQ_K // PROJ_TN)
    def _():
        for hh in range(nh):
            if hh < GQA_KV_HEADS:
                put(hh, rope(headnorm(head(acc, hh), hg_ref[3:4, :])))
            else:
                put(hh, head(acc, hh))

    @pl.when(j >= OFF_GA // PROJ_TN)
    def _():
        o_ref[...] = _sigmoid(acc + gb_ref[...]).astype(o_ref.dtype)


def _proj(x2d, seq_len, norm_g, w_in_bf, head_g, gate_bias, cos_t, sin_t):
    t = x2d.shape[0]
    tm = min(PROJ_TM, seq_len)
    assert t % tm == 0 and seq_len % tm == 0 and IN_WIDTH % PROJ_TN == 0
    assert OFF_GQ_K % PROJ_TN == 0 and (OFF_GQ_V + KV_W) % PROJ_TN == 0
    pos_blocks = seq_len // tm
    gate_tile0 = OFF_GA // PROJ_TN
    n_tiles = IN_WIDTH // PROJ_TN
    return pl.pallas_call(
        _proj_kernel,
        name="proj",
        grid=(t // tm, IN_WIDTH // PROJ_TN),
        in_specs=[
            pl.BlockSpec((tm, D_MODEL), lambda i, j: (i, 0)),
            pl.BlockSpec((1, D_MODEL), lambda i, j: (0, 0)),
            pl.BlockSpec((D_MODEL, PROJ_TN), lambda i, j: (0, j)),
            pl.BlockSpec((4, HEAD_DIM), lambda i, j: (0, 0)),
            pl.BlockSpec((1, PROJ_TN), lambda i, j: (0, jnp.maximum(j - gate_tile0, 0))),
            pl.BlockSpec((tm, HEAD_DIM), lambda i, j: (i % pos_blocks, 0)),
            pl.BlockSpec((tm, HEAD_DIM), lambda i, j: (i % pos_blocks, 0)),
        ],
        out_specs=pl.BlockSpec((tm, PROJ_TN), lambda i, j: (i, (j + 2 * D_MODEL // PROJ_TN) % n_tiles)),
        out_shape=jax.ShapeDtypeStruct((t, IN_WIDTH), BF16),
        scratch_shapes=[pltpu.VMEM((tm, D_MODEL), BF16)],
        compiler_params=pltpu.CompilerParams(
            dimension_semantics=("parallel", "arbitrary"),
            vmem_limit_bytes=VMEM_LIMIT_BYTES),
    )(x2d, norm_g, w_in_bf, head_g, gate_bias, cos_t, sin_t)


def _na_kernel(q_ref, k_ref, v_ref, z_ref, bias_ref, o_ref, *, rows):
    i = pl.program_id(2)
    nblk = rows // NA_QROWS
    wstart = jnp.clip(i * NA_QROWS - NA_KH // 2, 0, rows - NA_WROWS)
    start = pl.multiple_of(wstart * GRID_W, GRID_W)
    k = k_ref[pl.ds(start, NA_TK), :]
    v = v_ref[pl.ds(start, NA_TK), :]
    s = lax.dot_general(q_ref[...], k, (((1,), (1,)), ((), ())),
                        preferred_element_type=F32) * SCALE
    kind = jnp.where(i == 0, 0, jnp.where(i == nblk - 1, 2, 1))
    s = s + bias_ref[kind, 0]
    m = jnp.max(s, axis=-1, keepdims=True)
    p = jnp.exp(s - m)
    l = jnp.sum(p, axis=-1, keepdims=True)
    o = jnp.dot(p.astype(BF16), v, preferred_element_type=F32) / l
    o_ref[...] = (o * z_ref[...].astype(F32)).astype(o_ref.dtype)


def _na_bias_table(rpb):
    block_off = np.array([0, NA_KH // 2, NA_WROWS - NA_QROWS])[:, None, None]
    a = np.arange(NA_QROWS)[None, :, None]
    jw = np.arange(NA_WROWS)[None, None, :]
    first_key = np.array([0, -1, NA_WROWS - NA_KH])[:, None, None]
    first_key = np.where(first_key < 0, a, first_key)
    row_ok = (jw >= first_key) & (jw < first_key + NA_KH)
    dr = np.clip(jw - (block_off + a) + NA_KH - 1, 0, 2 * NA_KH - 2)
    c = np.arange(GRID_W)[:, None]
    kc = np.arange(GRID_W)[None, :]
    cs = np.clip(c - NA_KW // 2, 0, GRID_W - NA_KW)
    col_ok = (kc >= cs) & (kc < cs + NA_KW)
    dc = np.clip(kc - c + NA_KW - 1, 0, 2 * NA_KW - 2)
    tbl = rpb[:, dr][..., dc]
    ok = row_ok[:, :, :, None, None] & col_ok[None, None, None]
    tbl = jnp.where(ok[None], tbl.astype(F32), NEG_BIG)
    tbl = tbl.transpose(1, 0, 2, 4, 3, 5)
    return tbl.reshape(3, NA_HEADS, NA_TQ, NA_TK)


def _na(proj, batch, seq_len, bias_tbl):
    t = proj.shape[0]
    rows = seq_len // GRID_W
    assert rows % NA_QROWS == 0 and rows >= NA_WROWS + NA_QROWS
    nblk = rows // NA_QROWS
    cb = lambda off: _slab(off) // HEAD_DIM
    return pl.pallas_call(
        functools.partial(_na_kernel, rows=rows),
        name="na",
        grid=(NA_HEADS, batch, nblk),
        in_specs=[
            pl.BlockSpec((NA_TQ, HEAD_DIM), lambda h, b, i: (b * nblk + i, cb(OFF_NA_Q) + h)),
            pl.BlockSpec((seq_len, HEAD_DIM), lambda h, b, i: (b, cb(OFF_NA_K) + h)),
            pl.BlockSpec((seq_len, HEAD_DIM), lambda h, b, i: (b, cb(OFF_NA_V) + h)),
            pl.BlockSpec((NA_TQ, HEAD_DIM), lambda h, b, i: (b * nblk + i, cb(OFF_NA_Z) + h)),
            pl.BlockSpec((3, 1, NA_TQ, NA_TK), lambda h, b, i: (0, h, 0, 0)),
        ],
        out_specs=pl.BlockSpec((NA_TQ, HEAD_DIM), lambda h, b, i: (b * nblk + i, h)),
        out_shape=jax.ShapeDtypeStruct((t, NA_W), BF16),
        compiler_params=pltpu.CompilerParams(
            dimension_semantics=("parallel", "parallel", "arbitrary"),
            vmem_limit_bytes=VMEM_LIMIT_BYTES),
    )(proj, proj, proj, proj, bias_tbl)


def _gqa_kernel(q_ref, k_ref, v_ref, z_ref, o_ref, m_scr, l_scr, acc_scr, *, nk):
    tq = q_ref.shape[0]
    q = jnp.concatenate(
        [q_ref[:, g * HEAD_DIM:(g + 1) * HEAD_DIM] for g in range(GQA_GROUP)], axis=0)
    m_scr[...] = jnp.full(m_scr.shape, -jnp.inf, F32)
    l_scr[...] = jnp.zeros(l_scr.shape, F32)
    acc_scr[...] = jnp.zeros(acc_scr.shape, F32)

    def body(c, carry):
        start = pl.multiple_of(c * GQA_TK, GQA_TK)
        k = k_ref[pl.ds(start, GQA_TK), :]
        v = v_ref[pl.ds(start, GQA_TK), :]
        s = lax.dot_general(q, k, (((1,), (1,)), ((), ())),
                            preferred_element_type=F32) * SCALE
        m_old = m_scr[...]
        m_new = jnp.maximum(m_old, jnp.max(s, axis=-1, keepdims=True))
        alpha = jnp.exp(m_old - m_new)
        p = jnp.exp(s - m_new)
        l_scr[...] = alpha * l_scr[...] + jnp.sum(p, axis=-1, keepdims=True)
        acc_scr[...] = alpha * acc_scr[...] + jnp.dot(p.astype(BF16), v, preferred_element_type=F32)
        m_scr[...] = m_new
        return carry

    lax.fori_loop(0, nk, body, 0)
    o = acc_scr[...] / l_scr[...]
    for g in range(GQA_GROUP):
        cols = slice(g * HEAD_DIM, (g + 1) * HEAD_DIM)
        o_ref[:, cols] = (o[g * tq:(g + 1) * tq] * z_ref[:, cols].astype(F32)).astype(o_ref.dtype)


def _gqa(proj, batch, seq_len):
    t = proj.shape[0]
    assert seq_len % GQA_TQ == 0 and seq_len % GQA_TK == 0
    nq = seq_len // GQA_TQ
    gw = GQA_GROUP * HEAD_DIM
    return pl.pallas_call(
        functools.partial(_gqa_kernel, nk=seq_len // GQA_TK),
        name="gqa",
        grid=(batch, GQA_KV_HEADS, nq),
        in_specs=[
            pl.BlockSpec((GQA_TQ, gw), lambda b, g, i: (b * nq + i, _slab(OFF_GQ_Q) // gw + g)),
            pl.BlockSpec((seq_len, HEAD_DIM), lambda b, g, i: (b, _slab(OFF_GQ_K) // HEAD_DIM + g)),
            pl.BlockSpec((seq_len, HEAD_DIM), lambda b, g, i: (b, _slab(OFF_GQ_V) // HEAD_DIM + g)),
            pl.BlockSpec((GQA_TQ, gw), lambda b, g, i: (b * nq + i, _slab(OFF_GQ_Z) // gw + g)),
        ],
        out_specs=pl.BlockSpec((GQA_TQ, gw), lambda b, g, i: (b * nq + i, g)),
        out_shape=jax.ShapeDtypeStruct((t, GQ_W), BF16),
        scratch_shapes=[
            pltpu.VMEM((GQA_GROUP * GQA_TQ, 1), F32),
            pltpu.VMEM((GQA_GROUP * GQA_TQ, 1), F32),
            pltpu.VMEM((GQA_GROUP * GQA_TQ, HEAD_DIM), F32),
        ],
        compiler_params=pltpu.CompilerParams(
            dimension_semantics=("parallel", "parallel", "arbitrary"),
            vmem_limit_bytes=VMEM_LIMIT_BYTES),
    )(proj, proj, proj, proj)


def _tail_kernel(x_ref, oa_ref, ob_ref, ga_ref, gb_ref, wa_ref, wb_ref, wo_ref, y_ref):
    pa = jnp.dot(oa_ref[...], wa_ref[...], preferred_element_type=F32)
    pb = jnp.dot(ob_ref[...], wb_ref[...], preferred_element_type=F32)
    merged = ga_ref[...].astype(F32) * pa + gb_ref[...].astype(F32) * pb
    y_ref[...] = x_ref[...] + jnp.dot(merged.astype(BF16), wo_ref[...], preferred_element_type=F32)


def _tail(x2d, oa, ob, proj, wa_bf, wb_bf, wo_bf):
    t = x2d.shape[0]
    assert t % TAIL_TM == 0
    resident = dict(pipeline_mode=pl.Buffered(1))
    return pl.pallas_call(
        _tail_kernel,
        name="tail",
        grid=(t // TAIL_TM,),
        in_specs=[
            pl.BlockSpec((TAIL_TM, D_MODEL), lambda i: (i, 0)),
            pl.BlockSpec((TAIL_TM, NA_W), lambda i: (i, 0)),
            pl.BlockSpec((TAIL_TM, GQ_W), lambda i: (i, 0)),
            pl.BlockSpec((TAIL_TM, D_MODEL), lambda i: (i, _slab(OFF_GA) // D_MODEL)),
            pl.BlockSpec((TAIL_TM, D_MODEL), lambda i: (i, _slab(OFF_GB) // D_MODEL)),
            pl.BlockSpec((NA_W, D_MODEL), lambda i: (0, 0), **resident),
            pl.BlockSpec((GQ_W, D_MODEL), lambda i: (0, 0), **resident),
            pl.BlockSpec((D_MODEL, D_MODEL), lambda i: (0, 0), **resident),
        ],
        out_specs=pl.BlockSpec((TAIL_TM, D_MODEL), lambda i: (i, 0)),
        out_shape=jax.ShapeDtypeStruct((t, D_MODEL), F32),
        compiler_params=pltpu.CompilerParams(
            dimension_semantics=("parallel",),
            vmem_limit_bytes=VMEM_LIMIT_BYTES),
    )(x2d, oa, ob, proj, proj, wa_bf, wb_bf, wo_bf)


def _rope_tables(seq_len):
    pos = jnp.arange(seq_len)
    r = (pos // GRID_W).astype(F32)
    c = (pos % GRID_W).astype(F32)
    axis = HEAD_DIM // 2
    freqs = ROPE_THETA ** (-jnp.arange(0, axis, 2, dtype=F32) / axis)
    ang_r = r[:, None] * freqs[None, :]
    ang_c = c[:, None] * freqs[None, :]
    cos_t = jnp.concatenate([jnp.cos(ang_r), jnp.cos(ang_r), jnp.cos(ang_c), jnp.cos(ang_c)], axis=-1)
    sin_t = jnp.concatenate([-jnp.sin(ang_r), jnp.sin(ang_r), -jnp.sin(ang_c), jnp.sin(ang_c)], axis=-1)
    return cos_t, sin_t


def _layer(x, norm_g, w_in_bf, head_g, bias_tbl, wa_bf, wb_bf, gate_bias, wo_bf):
    batch, seq_len, _ = x.shape
    x2d = x.reshape(batch * seq_len, D_MODEL)
    cos_t, sin_t = _rope_tables(seq_len)
    proj = _proj(x2d, seq_len, norm_g, w_in_bf, head_g, gate_bias, cos_t, sin_t)
    oa = _na(proj, batch, seq_len, bias_tbl)
    ob = _gqa(proj, batch, seq_len)
    y = _tail(x2d, oa, ob, proj, wa_bf, wb_bf, wo_bf)
    return y.reshape(batch, seq_len, D_MODEL)


def kernel(x_prompt, x_sample, norm_g, w_in, na_q_g, na_k_g, na_rpb, gq_q_g, gq_k_g,
           w_branch_a, w_branch_b, gate_bias, w_out):
    depth = norm_g.shape[0]
    y_prompt, y_sample = x_prompt, x_sample
    for l in range(depth):
        head_g = jnp.stack([na_q_g[l], na_k_g[l], gq_q_g[l], gq_k_g[l]]).astype(F32)
        params = (norm_g[l][None, :].astype(F32), w_in[l].astype(BF16), head_g,
                  _na_bias_table(na_rpb[l]), w_branch_a[l].astype(BF16), w_branch_b[l].astype(BF16),
                  gate_bias[l][None, :].astype(F32), w_out[l].astype(BF16))
        y_prompt = _layer(y_prompt, *params)
        y_sample = _layer(y_sample, *params)
    return (y_prompt, y_sample)
```

```python
import functools

import numpy as np
import jax
import jax.numpy as jnp
from jax import lax
from jax.experimental import pallas as pl
from jax.experimental.pallas import tpu as pltpu

F32 = jnp.float32
BF16 = jnp.bfloat16

D_MODEL = 2048
HEAD_DIM = 128
GRID_W = 64
NA_HEADS = 8
NA_KH = 8
NA_KW = 16
GQA_HEADS = 8
GQA_KV_HEADS = 2
GQA_GROUP = GQA_HEADS // GQA_KV_HEADS
ROPE_THETA = 10000.0
EPS = 1e-6
SCALE = HEAD_DIM ** -0.5

NA_W = NA_HEADS * HEAD_DIM
GQ_W = GQA_HEADS * HEAD_DIM
KV_W = GQA_KV_HEADS * HEAD_DIM
OFF_NA_Q = 0
OFF_NA_K = OFF_NA_Q + NA_W
OFF_NA_V = OFF_NA_K + NA_W
OFF_NA_Z = OFF_NA_V + NA_W
OFF_GQ_Q = OFF_NA_Z + NA_W
OFF_GQ_K = OFF_GQ_Q + GQ_W
OFF_GQ_V = OFF_GQ_K + KV_W
OFF_GQ_Z = OFF_GQ_V + KV_W
OFF_GA = OFF_GQ_Z + GQ_W
OFF_GB = OFF_GA + D_MODEL
IN_WIDTH = OFF_GB + D_MODEL


def _slab(off):
    return (off + 2 * D_MODEL) % IN_WIDTH

VMEM_LIMIT_BYTES = 56 * 1024 * 1024

PROJ_TM = 1024
PROJ_TN = 512
NA_QROWS = 4
NA_WROWS = NA_QROWS + NA_KH - 1
NA_TQ = NA_QROWS * GRID_W
NA_TK = NA_WROWS * GRID_W
GQA_TQ = 256
GQA_TK = 512
GQA_ONES_ROWS = 16
EXP2_SCALE = SCALE * float(np.log2(np.e))
TAIL_TM = 256
NEG_BIG = -1e30


def _sigmoid(x):
    return 1.0 / (1.0 + jnp.exp(-x))


def _proj_kernel(x_ref, ng_ref, w_ref, hg_ref, gb_ref, cos_ref, sin_ref, o_ref, h_scr):
    j = pl.program_id(1)

    @pl.when(j == 0)
    def _():
        x = x_ref[...]
        ms = jnp.mean(x * x, axis=-1, keepdims=True)
        h_scr[...] = (x * lax.rsqrt(ms + EPS) * ng_ref[...]).astype(BF16)

    acc = jnp.dot(h_scr[...], w_ref[...], preferred_element_type=F32)
    nh = PROJ_TN // HEAD_DIM

    def head(a, hh):
        return a[:, hh * HEAD_DIM:(hh + 1) * HEAD_DIM]

    def put(hh, val):
        o_ref[:, hh * HEAD_DIM:(hh + 1) * HEAD_DIM] = val.astype(o_ref.dtype)

    def headnorm(xh, g):
        ms = jnp.mean(xh * xh, axis=-1, keepdims=True)
        return xh * lax.rsqrt(ms + EPS) * g

    def rope(y):
        lane = lax.broadcasted_iota(jnp.int32, y.shape, 1)
        first_half = (lane % (HEAD_DIM // 2)) < (HEAD_DIM // 4)
        partner = jnp.where(first_half,
                            pltpu.roll(y, HEAD_DIM - HEAD_DIM // 4, 1),
                            pltpu.roll(y, HEAD_DIM // 4, 1))
        return y * cos_ref[...] + partner * sin_ref[...]

    def tiles(off, width):
        return off // PROJ_TN, (off + width) // PROJ_TN

    def in_range(lo_hi):
        return jnp.logical_and(j >= lo_hi[0], j < lo_hi[1])

    @pl.when(in_range(tiles(OFF_NA_Q, NA_W)))
    def _():
        for hh in range(nh):
            put(hh, headnorm(head(acc, hh), hg_ref[0:1, :]))

    @pl.when(in_range(tiles(OFF_NA_K, NA_W)))
    def _():
        for hh in range(nh):
            put(hh, headnorm(head(acc, hh), hg_ref[1:2, :]))

    @pl.when(in_range(tiles(OFF_NA_V, NA_W)))
    def _():
        o_ref[...] = acc.astype(o_ref.dtype)

    @pl.when(jnp.logical_or(in_range(tiles(OFF_NA_Z, NA_W)), in_range(tiles(OFF_GQ_Z, GQ_W))))
    def _():
        o_ref[...] = (acc * _sigmoid(acc)).astype(o_ref.dtype)

    @pl.when(in_range(tiles(OFF_GQ_Q, GQ_W)))
    def _():
        for hh in range(nh):
            put(hh, rope(headnorm(head(acc, hh), hg_ref[2:3, :])))

    @pl.when(j == OFF_GQ_K // PROJ_TN)
    def _():
        for hh in range(nh):
            if hh < GQA_KV_HEADS:
                put(hh, rope(headnorm(head(acc, hh), hg_ref[3:4, :])))
            else:
                put(hh, head(acc, hh))

    @pl.when(j >= OFF_GA // PROJ_TN)
    def _():
        o_ref[...] = _sigmoid(acc + gb_ref[...]).astype(o_ref.dtype)


def _proj(x2d, seq_len, norm_g, w_in_bf, head_g, gate_bias, cos_t, sin_t):
    t = x2d.shape[0]
    tm = min(PROJ_TM, seq_len)
    assert t % tm == 0 and seq_len % tm == 0 and IN_WIDTH % PROJ_TN == 0
    assert OFF_GQ_K % PROJ_TN == 0 and (OFF_GQ_V + KV_W) % PROJ_TN == 0
    pos_blocks = seq_len // tm
    gate_tile0 = OFF_GA // PROJ_TN
    n_tiles = IN_WIDTH // PROJ_TN
    return pl.pallas_call(
        _proj_kernel,
        name="proj",
        grid=(t // tm, IN_WIDTH // PROJ_TN),
        in_specs=[
            pl.BlockSpec((tm, D_MODEL), lambda i, j: (i, 0)),
            pl.BlockSpec((1, D_MODEL), lambda i, j: (0, 0)),
            pl.BlockSpec((D_MODEL, PROJ_TN), lambda i, j: (0, j)),
            pl.BlockSpec((4, HEAD_DIM), lambda i, j: (0, 0)),
            pl.BlockSpec((1, PROJ_TN), lambda i, j: (0, jnp.maximum(j - gate_tile0, 0))),
            pl.BlockSpec((tm, HEAD_DIM), lambda i, j: (i % pos_blocks, 0)),
            pl.BlockSpec((tm, HEAD_DIM), lambda i, j: (i % pos_blocks, 0)),
        ],
        out_specs=pl.BlockSpec((tm, PROJ_TN), lambda i, j: (i, (j + 2 * D_MODEL // PROJ_TN) % n_tiles)),
        out_shape=jax.ShapeDtypeStruct((t, IN_WIDTH), BF16),
        scratch_shapes=[pltpu.VMEM((tm, D_MODEL), BF16)],
        compiler_params=pltpu.CompilerParams(
            dimension_semantics=("parallel", "arbitrary"),
            vmem_limit_bytes=VMEM_LIMIT_BYTES),
    )(x2d, norm_g, w_in_bf, head_g, gate_bias, cos_t, sin_t)


def _na_kernel(q_ref, k_ref, v_ref, z_ref, bias_ref, o_ref, *, rows):
    i = pl.program_id(2)
    nblk = rows // NA_QROWS
    wstart = jnp.clip(i * NA_QROWS - NA_KH // 2, 0, rows - NA_WROWS)
    start = pl.multiple_of(wstart * GRID_W, GRID_W)
    k = k_ref[pl.ds(start, NA_TK), :]
    v = v_ref[pl.ds(start, NA_TK), :]
    s = lax.dot_general(q_ref[...], k, (((1,), (1,)), ((), ())),
                        preferred_element_type=F32) * SCALE
    kind = jnp.where(i == 0, 0, jnp.where(i == nblk - 1, 2, 1))
    s = s + bias_ref[kind, 0]
    m = jnp.max(s, axis=-1, keepdims=True)
    p = jnp.exp(s - m)
    l = jnp.sum(p, axis=-1, keepdims=True)
    o = jnp.dot(p.astype(BF16), v, preferred_element_type=F32) / l
    o_ref[...] = (o * z_ref[...].astype(F32)).astype(o_ref.dtype)


def _na_bias_table(rpb):
    block_off = np.array([0, NA_KH // 2, NA_WROWS - NA_QROWS])[:, None, None]
    a = np.arange(NA_QROWS)[None, :, None]
    jw = np.arange(NA_WROWS)[None, None, :]
    first_key = np.array([0, -1, NA_WROWS - NA_KH])[:, None, None]
    first_key = np.where(first_key < 0, a, first_key)
    row_ok = (jw >= first_key) & (jw < first_key + NA_KH)
    dr = np.clip(jw - (block_off + a) + NA_KH - 1, 0, 2 * NA_KH - 2)
    c = np.arange(GRID_W)[:, None]
    kc = np.arange(GRID_W)[None, :]
    cs = np.clip(c - NA_KW // 2, 0, GRID_W - NA_KW)
    col_ok = (kc >= cs) & (kc < cs + NA_KW)
    dc = np.clip(kc - c + NA_KW - 1, 0, 2 * NA_KW - 2)
    tbl = rpb[:, dr][..., dc]
    ok = row_ok[:, :, :, None, None] & col_ok[None, None, None]
    tbl = jnp.where(ok[None], tbl.astype(F32), NEG_BIG)
    tbl = tbl.transpose(1, 0, 2, 4, 3, 5)
    return tbl.reshape(3, NA_HEADS, NA_TQ, NA_TK)


def _na(proj, batch, seq_len, bias_tbl):
    t = proj.shape[0]
    rows = seq_len // GRID_W
    assert rows % NA_QROWS == 0 and rows >= NA_WROWS + NA_QROWS
    nblk = rows // NA_QROWS
    cb = lambda off: _slab(off) // HEAD_DIM
    return pl.pallas_call(
        functools.partial(_na_kernel, rows=rows),
        name="na",
        grid=(NA_HEADS, batch, nblk),
        in_specs=[
            pl.BlockSpec((NA_TQ, HEAD_DIM), lambda h, b, i: (b * nblk + i, cb(OFF_NA_Q) + h)),
            pl.BlockSpec((seq_len, HEAD_DIM), lambda h, b, i: (b, cb(OFF_NA_K) + h)),
            pl.BlockSpec((seq_len, HEAD_DIM), lambda h, b, i: (b, cb(OFF_NA_V) + h)),
            pl.BlockSpec((NA_TQ, HEAD_DIM), lambda h, b, i: (b * nblk + i, cb(OFF_NA_Z) + h)),
            pl.BlockSpec((3, 1, NA_TQ, NA_TK), lambda h, b, i: (0, h, 0, 0)),
        ],
        out_specs=pl.BlockSpec((NA_TQ, HEAD_DIM), lambda h, b, i: (b * nblk + i, h)),
        out_shape=jax.ShapeDtypeStruct((t, NA_W), BF16),
        compiler_params=pltpu.CompilerParams(
            dimension_semantics=("parallel", "parallel", "arbitrary"),
            vmem_limit_bytes=VMEM_LIMIT_BYTES),
    )(proj, proj, proj, proj, bias_tbl)


def _gqa_kernel(q_ref, k_ref, v_ref, z_ref, o_ref, qt_scr, vt_scr, m_scr, acc_scr, sa_scr, sb_scr,
                *, nk):
    i = pl.program_id(2)

    @pl.when(i == 0)
    def _():
        def fill(c, carry):
            start = pl.multiple_of(c * GQA_TK, GQA_TK)
            v = v_ref[pl.ds(start, GQA_TK), :].astype(F32)
            vt_scr[c, 0:HEAD_DIM, :] = v.T.astype(BF16)
            vt_scr[c, HEAD_DIM:, :] = jnp.ones((GQA_ONES_ROWS, GQA_TK), BF16)
            return carry
        lax.fori_loop(0, nk, fill, 0)

    for g in range(GQA_GROUP):
        qh = q_ref[:, g * HEAD_DIM:(g + 1) * HEAD_DIM].astype(F32)
        qt_scr[g] = qh.T.astype(BF16)
    m_scr[...] = jnp.full(m_scr.shape, -jnp.inf, F32)
    acc_scr[...] = jnp.zeros(acc_scr.shape, F32)

    def scores(c, s_buf):
        start = pl.multiple_of(c * GQA_TK, GQA_TK)
        k = k_ref[pl.ds(start, GQA_TK), :]
        for g in range(GQA_GROUP):
            s_buf[g] = jnp.dot(k, qt_scr[g], preferred_element_type=F32)

    def accumulate(c, s_buf):
        vt = vt_scr[c]
        for g in range(GQA_GROUP):
            s = s_buf[g]
            m_old = m_scr[g]
            m_new = jnp.maximum(m_old, jnp.max(s, axis=0, keepdims=True))
            alpha = jnp.exp2(EXP2_SCALE * (m_old - m_new))
            p = jnp.exp2(EXP2_SCALE * (s - m_new)).astype(BF16)
            acc_scr[g] = alpha * acc_scr[g] + jnp.dot(vt, p, preferred_element_type=F32)
            m_scr[g] = m_new

    scores(0, sa_scr)

    def pair(j, carry):
        scores(2 * j + 1, sb_scr)
        accumulate(2 * j, sa_scr)
        scores(2 * j + 2, sa_scr)
        accumulate(2 * j + 1, sb_scr)
        return carry

    lax.fori_loop(0, nk // 2 - 1, pair, 0)
    scores(nk - 1, sb_scr)
    accumulate(nk - 2, sa_scr)
    accumulate(nk - 1, sb_scr)
    for g in range(GQA_GROUP):
        cols = slice(g * HEAD_DIM, (g + 1) * HEAD_DIM)
        acc = acc_scr[g]
        o = (acc[0:HEAD_DIM] / acc[HEAD_DIM:HEAD_DIM + 1]).T
        o_ref[:, cols] = (o * z_ref[:, cols].astype(F32)).astype(o_ref.dtype)


def _gqa(proj, batch, seq_len):
    t = proj.shape[0]
    assert seq_len % GQA_TQ == 0 and seq_len % (2 * GQA_TK) == 0
    nq = seq_len // GQA_TQ
    gw = GQA_GROUP * HEAD_DIM
    return pl.pallas_call(
        functools.partial(_gqa_kernel, nk=seq_len // GQA_TK),
        name="gqa",
        grid=(batch, GQA_KV_HEADS, nq),
        in_specs=[
            pl.BlockSpec((GQA_TQ, gw), lambda b, g, i: (b * nq + i, _slab(OFF_GQ_Q) // gw + g)),
            pl.BlockSpec((seq_len, HEAD_DIM), lambda b, g, i: (b, _slab(OFF_GQ_K) // HEAD_DIM + g)),
            pl.BlockSpec((seq_len, HEAD_DIM), lambda b, g, i: (b, _slab(OFF_GQ_V) // HEAD_DIM + g)),
            pl.BlockSpec((GQA_TQ, gw), lambda b, g, i: (b * nq + i, _slab(OFF_GQ_Z) // gw + g)),
        ],
        out_specs=pl.BlockSpec((GQA_TQ, gw), lambda b, g, i: (b * nq + i, g)),
        out_shape=jax.ShapeDtypeStruct((t, GQ_W), BF16),
        scratch_shapes=[
            pltpu.VMEM((GQA_GROUP, HEAD_DIM, GQA_TQ), BF16),
            pltpu.VMEM((seq_len // GQA_TK, HEAD_DIM + GQA_ONES_ROWS, GQA_TK), BF16),
            pltpu.VMEM((GQA_GROUP, 1, GQA_TQ), F32),
            pltpu.VMEM((GQA_GROUP, HEAD_DIM + GQA_ONES_ROWS, GQA_TQ), F32),
            pltpu.VMEM((GQA_GROUP, GQA_TK, GQA_TQ), F32),
            pltpu.VMEM((GQA_GROUP, GQA_TK, GQA_TQ), F32),
        ],
        compiler_params=pltpu.CompilerParams(
            dimension_semantics=("parallel", "parallel", "arbitrary"),
            vmem_limit_bytes=VMEM_LIMIT_BYTES),
    )(proj, proj, proj, proj)


def _tail_kernel(x_ref, oa_ref, ob_ref, ga_ref, gb_ref, wa_ref, wb_ref, wo_ref, y_ref):
    pa = jnp.dot(oa_ref[...], wa_ref[...], preferred_element_type=F32)
    pb = jnp.dot(ob_ref[...], wb_ref[...], preferred_element_type=F32)
    merged = ga_ref[...].astype(F32) * pa + gb_ref[...].astype(F32) * pb
    y_ref[...] = x_ref[...] + jnp.dot(merged.astype(BF16), wo_ref[...], preferred_element_type=F32)


def _tail(x2d, oa, ob, proj, wa_bf, wb_bf, wo_bf):
    t = x2d.shape[0]
    assert t % TAIL_TM == 0
    resident = dict(pipeline_mode=pl.Buffered(1))
    return pl.pallas_call(
        _tail_kernel,
        name="tail",
        grid=(t // TAIL_TM,),
        in_specs=[
            pl.BlockSpec((TAIL_TM, D_MODEL), lambda i: (i, 0)),
            pl.BlockSpec((TAIL_TM, NA_W), lambda i: (i, 0)),
            pl.BlockSpec((TAIL_TM, GQ_W), lambda i: (i, 0)),
            pl.BlockSpec((TAIL_TM, D_MODEL), lambda i: (i, _slab(OFF_GA) // D_MODEL)),
            pl.BlockSpec((TAIL_TM, D_MODEL), lambda i: (i, _slab(OFF_GB) // D_MODEL)),
            pl.BlockSpec((NA_W, D_MODEL), lambda i: (0, 0), **resident),
            pl.BlockSpec((GQ_W, D_MODEL), lambda i: (0, 0), **resident),
            pl.BlockSpec((D_MODEL, D_MODEL), lambda i: (0, 0), **resident),
        ],
        out_specs=pl.BlockSpec((TAIL_TM, D_MODEL), lambda i: (i, 0)),
        out_shape=jax.ShapeDtypeStruct((t, D_MODEL), F32),
        compiler_params=pltpu.CompilerParams(
            dimension_semantics=("parallel",),
            vmem_limit_bytes=VMEM_LIMIT_BYTES),
    )(x2d, oa, ob, proj, proj, wa_bf, wb_bf, wo_bf)


def _rope_tables(seq_len):
    pos = jnp.arange(seq_len)
    r = (pos // GRID_W).astype(F32)
    c = (pos % GRID_W).astype(F32)
    axis = HEAD_DIM // 2
    freqs = ROPE_THETA ** (-jnp.arange(0, axis, 2, dtype=F32) / axis)
    ang_r = r[:, None] * freqs[None, :]
    ang_c = c[:, None] * freqs[None, :]
    cos_t = jnp.concatenate([jnp.cos(ang_r), jnp.cos(ang_r), jnp.cos(ang_c), jnp.cos(ang_c)], axis=-1)
    sin_t = jnp.concatenate([-jnp.sin(ang_r), jnp.sin(ang_r), -jnp.sin(ang_c), jnp.sin(ang_c)], axis=-1)
    return cos_t, sin_t


def _layer(x, norm_g, w_in_bf, head_g, bias_tbl, wa_bf, wb_bf, gate_bias, wo_bf):
    batch, seq_len, _ = x.shape
    x2d = x.reshape(batch * seq_len, D_MODEL)
    cos_t, sin_t = _rope_tables(seq_len)
    proj = _proj(x2d, seq_len, norm_g, w_in_bf, head_g, gate_bias, cos_t, sin_t)
    oa = _na(proj, batch, seq_len, bias_tbl)
    ob = _gqa(proj, batch, seq_len)
    y = _tail(x2d, oa, ob, proj, wa_bf, wb_bf, wo_bf)
    return y.reshape(batch, seq_len, D_MODEL)


def kernel(x_prompt, x_sample, norm_g, w_in, na_q_g, na_k_g, na_rpb, gq_q_g, gq_k_g,
           w_branch_a, w_branch_b, gate_bias, w_out):
    depth = norm_g.shape[0]
    y_prompt, y_sample = x_prompt, x_sample
    for l in range(depth):
        head_g = jnp.stack([na_q_g[l], na_k_g[l], gq_q_g[l], gq_k_g[l]]).astype(F32)
        params = (norm_g[l][None, :].astype(F32), w_in[l].astype(BF16), head_g,
                  _na_bias_table(na_rpb[l]), w_branch_a[l].astype(BF16), w_branch_b[l].astype(BF16),
                  gate_bias[l][None, :].astype(F32), w_out[l].astype(BF16))
        y_prompt = _layer(y_prompt, *params)
        y_sample = _layer(y_sample, *params)
    return (y_prompt, y_sample)
```

```python
import functools

import numpy as np
import jax
import jax.numpy as jnp
from jax import lax
from jax.experimental import pallas as pl
from jax.experimental.pallas import tpu as pltpu

F32 = jnp.float32
BF16 = jnp.bfloat16

D_MODEL = 2048
HEAD_DIM = 128
GRID_W = 64
NA_HEADS = 8
NA_KH = 8
NA_KW = 16
GQA_HEADS = 8
GQA_KV_HEADS = 2
GQA_GROUP = GQA_HEADS // GQA_KV_HEADS
ROPE_THETA = 10000.0
EPS = 1e-6
SCALE = HEAD_DIM ** -0.5

NA_W = NA_HEADS * HEAD_DIM
GQ_W = GQA_HEADS * HEAD_DIM
KV_W = GQA_KV_HEADS * HEAD_DIM
OFF_NA_Q = 0
OFF_NA_K = OFF_NA_Q + NA_W
OFF_NA_V = OFF_NA_K + NA_W
OFF_NA_Z = OFF_NA_V + NA_W
OFF_GQ_Q = OFF_NA_Z + NA_W
OFF_GQ_K = OFF_GQ_Q + GQ_W
OFF_GQ_V = OFF_GQ_K + KV_W
OFF_GQ_Z = OFF_GQ_V + KV_W
OFF_GA = OFF_GQ_Z + GQ_W
OFF_GB = OFF_GA + D_MODEL
IN_WIDTH = OFF_GB + D_MODEL


def _slab(off):
    return (off + 2 * D_MODEL) % IN_WIDTH

VMEM_LIMIT_BYTES = 56 * 1024 * 1024

PROJ_TM = 1024
PROJ_TN = 512
NA_QROWS = 4
NA_TQ = NA_QROWS * GRID_W
NA_WCHUNKS = 3
NA_WROWS = NA_WCHUNKS * NA_QROWS
NA_TK = NA_WROWS * GRID_W
NA_NB = 4
NA_ONES_ROWS = 16
GQA_TQ = 256
GQA_TK = 512
GQA_ONES_ROWS = 16
EXP2_SCALE = SCALE * float(np.log2(np.e))
TAIL_TM = 256
NEG_BIG = -1e30


def _sigmoid(x):
    return 0.5 * jnp.tanh(0.5 * x) + 0.5


def _proj_kernel(x_ref, ng_ref, w_ref, hg_ref, gb_ref, o_ref, h_scr):
    j = pl.program_id(1)

    @pl.when(j == 0)
    def _():
        x = x_ref[...]
        ms = jnp.mean(x * x, axis=-1, keepdims=True)
        h_scr[...] = (x * lax.rsqrt(ms + EPS) * ng_ref[...]).astype(BF16)

    nh = PROJ_TN // HEAD_DIM

    def run(epilogue):
        for pp in range(nh // 2):
            cols = slice(2 * pp * HEAD_DIM, 2 * (pp + 1) * HEAD_DIM)
            acc = jnp.dot(h_scr[...], w_ref[:, cols], preferred_element_type=F32)
            for hh in (2 * pp, 2 * pp + 1):
                hcols = slice(hh * HEAD_DIM, (hh + 1) * HEAD_DIM)
                val = epilogue(hh, acc[:, (hh % 2) * HEAD_DIM:(hh % 2 + 1) * HEAD_DIM])
                o_ref[:, hcols] = val.astype(o_ref.dtype)

    def headnorm(xh, g):
        ms = jnp.mean(xh * xh, axis=-1, keepdims=True)
        return xh * lax.rsqrt(ms + EPS) * g

    def tiles(off, width):
        return off // PROJ_TN, (off + width) // PROJ_TN

    def in_range(lo_hi):
        return jnp.logical_and(j >= lo_hi[0], j < lo_hi[1])

    @pl.when(jnp.logical_or(in_range(tiles(OFF_NA_Q, 2 * NA_W)), in_range(tiles(OFF_GQ_Q, GQ_W))))
    def _():
        gain_row = jnp.where(j >= OFF_GQ_Q // PROJ_TN, 2, (j >= OFF_NA_K // PROJ_TN).astype(jnp.int32))
        g = hg_ref[pl.ds(gain_row, 1), :]
        run(lambda hh, a: headnorm(a, g))

    @pl.when(in_range(tiles(OFF_NA_V, NA_W)))
    def _():
        run(lambda hh, a: a)

    @pl.when(jnp.logical_or(in_range(tiles(OFF_NA_Z, NA_W)), in_range(tiles(OFF_GQ_Z, GQ_W))))
    def _():
        run(lambda hh, a: a * _sigmoid(a))

    @pl.when(j == OFF_GQ_K // PROJ_TN)
    def _():
        run(lambda hh, a: headnorm(a, hg_ref[3:4, :]) if hh < GQA_KV_HEADS else a)

    @pl.when(j >= OFF_GA // PROJ_TN)
    def _():
        run(lambda hh, a: _sigmoid(a + gb_ref[:, hh * HEAD_DIM:(hh + 1) * HEAD_DIM]))


def _proj(x2d, norm_g, w_in_bf, head_g, gate_bias):
    t = x2d.shape[0]
    tm = PROJ_TM
    assert t % tm == 0 and IN_WIDTH % PROJ_TN == 0
    assert OFF_GQ_K % PROJ_TN == 0 and (OFF_GQ_V + KV_W) % PROJ_TN == 0
    gate_tile0 = OFF_GA // PROJ_TN
    n_tiles = IN_WIDTH // PROJ_TN
    return pl.pallas_call(
        _proj_kernel,
        name="proj",
        grid=(t // tm, IN_WIDTH // PROJ_TN),
        in_specs=[
            pl.BlockSpec((tm, D_MODEL), lambda i, j: (i, 0)),
            pl.BlockSpec((1, D_MODEL), lambda i, j: (0, 0)),
            pl.BlockSpec((D_MODEL, PROJ_TN), lambda i, j: (0, j)),
            pl.BlockSpec((4, HEAD_DIM), lambda i, j: (0, 0)),
            pl.BlockSpec((1, PROJ_TN), lambda i, j: (0, jnp.maximum(j - gate_tile0, 0))),
        ],
        out_specs=pl.BlockSpec((tm, PROJ_TN), lambda i, j: (i, (j + 2 * D_MODEL // PROJ_TN) % n_tiles)),
        out_shape=jax.ShapeDtypeStruct((t, IN_WIDTH), BF16),
        scratch_shapes=[pltpu.VMEM((tm, D_MODEL), BF16)],
        compiler_params=pltpu.CompilerParams(
            dimension_semantics=("parallel", "arbitrary"),
            vmem_limit_bytes=VMEM_LIMIT_BYTES),
    )(x2d, norm_g, w_in_bf, head_g, gate_bias)


def _na_kernel(q_ref, k_ref, v_ref, z_ref, bias_ref, o_ref, vt_scr, s_scr, *, nblk):
    i = pl.program_id(2)

    @pl.when(i == 0)
    def _():
        def fill(c, carry):
            rows = pl.ds(pl.multiple_of(c * NA_TQ, NA_TQ), NA_TQ)
            vt_scr[c, 0:HEAD_DIM, :] = v_ref[rows, :].astype(F32).T.astype(BF16)
            vt_scr[c, HEAD_DIM:, :] = jnp.ones((NA_ONES_ROWS, NA_TQ), BF16)
            return carry
        lax.fori_loop(0, nblk, fill, 0)

    def first_chunk(nb):
        return jnp.clip(nb - 1, 0, nblk - NA_WCHUNKS)

    for jb in range(NA_NB):
        start = pl.multiple_of(first_chunk(i * NA_NB + jb) * NA_TQ, NA_TQ)
        k = k_ref[pl.ds(start, NA_TK), :]
        q = q_ref[jb * NA_TQ:(jb + 1) * NA_TQ, :]
        s_scr[jb] = lax.dot_general(k, q, (((1,), (1,)), ((), ())), preferred_element_type=F32)

    for jb in range(NA_NB):
        nb = i * NA_NB + jb
        chunk0 = first_chunk(nb)
        kind = jnp.where(nb == 0, 0, jnp.where(nb == nblk - 1, 2, 1))
        logits = EXP2_SCALE * s_scr[jb] + bias_ref[kind, 0]
        m = jnp.max(logits, axis=0, keepdims=True)
        p = jnp.exp2(logits - m).astype(BF16)
        pv = jnp.dot(vt_scr[chunk0], p[0:NA_TQ], preferred_element_type=F32)
        for w in range(1, NA_WCHUNKS):
            pv += jnp.dot(vt_scr[chunk0 + w], p[w * NA_TQ:(w + 1) * NA_TQ],
                          preferred_element_type=F32)
        o = (pv[0:HEAD_DIM] / pv[HEAD_DIM:HEAD_DIM + 1]).T
        rows = slice(jb * NA_TQ, (jb + 1) * NA_TQ)
        o_ref[rows, :] = (o * z_ref[rows, :].astype(F32)).astype(o_ref.dtype)


def _na_bias_table(rpb):
    c = np.arange(GRID_W)[:, None]
    kc = np.arange(GRID_W)[None, :]
    cs = np.clip(c - NA_KW // 2, 0, GRID_W - NA_KW)
    col_ok = (kc >= cs) & (kc < cs + NA_KW)
    dc = kc - c + NA_KW - 1
    onehot = ((dc[None] == np.arange(2 * NA_KW - 1)[:, None, None]) & col_ok[None]).astype(np.float32)
    t1 = jnp.einsum('hrd,dck->hrck', rpb.astype(F32), onehot, precision=lax.Precision.HIGHEST)
    block_off = (0, NA_KH // 2, NA_WROWS - NA_QROWS)
    pad = NA_QROWS
    t1 = jnp.pad(t1, ((0, 0), (pad, pad), (0, 0), (0, 0)))
    tbl = jnp.stack([
        jnp.stack([lax.slice_in_dim(t1, NA_KH - 1 - off - a + pad, NA_KH - 1 - off - a + pad + NA_WROWS,
                                    axis=1) for a in range(NA_QROWS)])
        for off in block_off])
    a = np.arange(NA_QROWS)[None, :, None]
    jw = np.arange(NA_WROWS)[None, None, :]
    first_key = np.stack([np.zeros_like(a), a, np.full_like(a, NA_WROWS - NA_KH)])[:, 0]
    row_ok = (jw >= first_key) & (jw < first_key + NA_KH)
    ok = row_ok[:, :, None, :, None, None] & col_ok[None, None, None, None]
    tbl = jnp.where(ok, tbl * float(np.log2(np.e)), NEG_BIG)
    tbl = tbl.transpose(0, 2, 3, 5, 1, 4)
    return tbl.reshape(3, NA_HEADS, NA_TK, NA_TQ)


def _na(proj, batch, seq_len, bias_tbl):
    t = proj.shape[0]
    rows = seq_len // GRID_W
    nblk = rows // NA_QROWS
    assert rows % (NA_QROWS * NA_NB) == 0 and nblk >= NA_WCHUNKS
    nstep = nblk // NA_NB
    tq = NA_NB * NA_TQ
    cb = lambda off: _slab(off) // HEAD_DIM
    return pl.pallas_call(
        functools.partial(_na_kernel, nblk=nblk),
        name="na",
        grid=(NA_HEADS, batch, nstep),
        in_specs=[
            pl.BlockSpec((tq, HEAD_DIM), lambda h, b, i: (b * nstep + i, cb(OFF_NA_Q) + h)),
            pl.BlockSpec((seq_len, HEAD_DIM), lambda h, b, i: (b, cb(OFF_NA_K) + h)),
            pl.BlockSpec((seq_len, HEAD_DIM), lambda h, b, i: (b, cb(OFF_NA_V) + h)),
            pl.BlockSpec((tq, HEAD_DIM), lambda h, b, i: (b * nstep + i, cb(OFF_NA_Z) + h)),
            pl.BlockSpec((3, 1, NA_TK, NA_TQ), lambda h, b, i: (0, h, 0, 0)),
        ],
        out_specs=pl.BlockSpec((tq, HEAD_DIM), lambda h, b, i: (b * nstep + i, h)),
        out_shape=jax.ShapeDtypeStruct((t, NA_W), BF16),
        scratch_shapes=[
            pltpu.VMEM((nblk, HEAD_DIM + NA_ONES_ROWS, NA_TQ), BF16),
            pltpu.VMEM((NA_NB, NA_TK, NA_TQ), F32),
        ],
        compiler_params=pltpu.CompilerParams(
            dimension_semantics=("parallel", "parallel", "arbitrary"),
            vmem_limit_bytes=VMEM_LIMIT_BYTES),
    )(proj, proj, proj, proj, bias_tbl)


def _gqa_kernel(q_ref, k_ref, v_ref, z_ref, cos_ref, sin_ref, cost_ref, sint_ref, o_ref,
                qt_scr, k_scr, vt_scr, m_scr, acc_scr, sa_scr, sb_scr, *, nk):
    i = pl.program_id(2)
    quarter = HEAD_DIM // 4

    @pl.when(i == 0)
    def _():
        def fill(c, carry):
            start = pl.multiple_of(c * GQA_TK, GQA_TK)
            rows = pl.ds(start, GQA_TK)
            v = v_ref[rows, :].astype(F32)
            vt_scr[c, 0:HEAD_DIM, :] = v.T.astype(BF16)
            vt_scr[c, HEAD_DIM:, :] = jnp.ones((GQA_ONES_ROWS, GQA_TK), BF16)
            kk = k_ref[rows, :].astype(F32)
            lane = lax.broadcasted_iota(jnp.int32, kk.shape, 1)
            partner = jnp.where((lane % (2 * quarter)) < quarter,
                                pltpu.roll(kk, HEAD_DIM - quarter, 1),
                                pltpu.roll(kk, quarter, 1))
            k_scr[rows, :] = (kk * cos_ref[rows, :] + partner * sin_ref[rows, :]).astype(BF16)
            return carry
        lax.fori_loop(0, nk, fill, 0)

    for g in range(GQA_GROUP):
        qt = q_ref[:, g * HEAD_DIM:(g + 1) * HEAD_DIM].astype(F32).T
        partner = jnp.concatenate([qt[quarter:2 * quarter], qt[0:quarter],
                                   qt[3 * quarter:], qt[2 * quarter:3 * quarter]], axis=0)
        qt_scr[g] = (qt * cost_ref[...] + partner * sint_ref[...]).astype(BF16)
    m_scr[...] = jnp.full(m_scr.shape, -jnp.inf, F32)
    acc_scr[...] = jnp.zeros(acc_scr.shape, F32)

    def scores(c, s_buf):
        start = pl.multiple_of(c * GQA_TK, GQA_TK)
        k = k_scr[pl.ds(start, GQA_TK), :]
        for g in range(GQA_GROUP):
            s_buf[g] = jnp.dot(k, qt_scr[g], preferred_element_type=F32)

    def accumulate(c, s_buf):
        vt = vt_scr[c]
        for g in range(GQA_GROUP):
            s = s_buf[g]
            m_old = m_scr[g]
            m_new = jnp.maximum(m_old, jnp.max(s, axis=0, keepdims=True))
            alpha = jnp.exp2(EXP2_SCALE * (m_old - m_new))
            p = jnp.exp2(EXP2_SCALE * (s - m_new)).astype(BF16)
            acc_scr[g] = alpha * acc_scr[g] + jnp.dot(vt, p, preferred_element_type=F32)
            m_scr[g] = m_new

    scores(0, sa_scr)

    def pair(j, carry):
        scores(2 * j + 1, sb_scr)
        accumulate(2 * j, sa_scr)
        scores(2 * j + 2, sa_scr)
        accumulate(2 * j + 1, sb_scr)
        return carry

    lax.fori_loop(0, nk // 2 - 1, pair, 0)
    scores(nk - 1, sb_scr)
    accumulate(nk - 2, sa_scr)
    accumulate(nk - 1, sb_scr)
    for g in range(GQA_GROUP):
        cols = slice(g * HEAD_DIM, (g + 1) * HEAD_DIM)
        acc = acc_scr[g]
        o = (acc[0:HEAD_DIM] / acc[HEAD_DIM:HEAD_DIM + 1]).T
        o_ref[:, cols] = (o * z_ref[:, cols].astype(F32)).astype(o_ref.dtype)


def _gqa(proj, batch, seq_len, cos_t, sin_t):
    t = proj.shape[0]
    assert seq_len % GQA_TQ == 0 and seq_len % (2 * GQA_TK) == 0
    nq = seq_len // GQA_TQ
    gw = GQA_GROUP * HEAD_DIM
    return pl.pallas_call(
        functools.partial(_gqa_kernel, nk=seq_len // GQA_TK),
        name="gqa",
        grid=(batch, GQA_KV_HEADS, nq),
        in_specs=[
            pl.BlockSpec((GQA_TQ, gw), lambda b, g, i: (b * nq + i, _slab(OFF_GQ_Q) // gw + g)),
            pl.BlockSpec((seq_len, HEAD_DIM), lambda b, g, i: (b, _slab(OFF_GQ_K) // HEAD_DIM + g)),
            pl.BlockSpec((seq_len, HEAD_DIM), lambda b, g, i: (b, _slab(OFF_GQ_V) // HEAD_DIM + g)),
            pl.BlockSpec((GQA_TQ, gw), lambda b, g, i: (b * nq + i, _slab(OFF_GQ_Z) // gw + g)),
            pl.BlockSpec((seq_len, HEAD_DIM), lambda b, g, i: (0, 0), pipeline_mode=pl.Buffered(1)),
            pl.BlockSpec((seq_len, HEAD_DIM), lambda b, g, i: (0, 0), pipeline_mode=pl.Buffered(1)),
            pl.BlockSpec((HEAD_DIM, GQA_TQ), lambda b, g, i: (0, i)),
            pl.BlockSpec((HEAD_DIM, GQA_TQ), lambda b, g, i: (0, i)),
        ],
        out_specs=pl.BlockSpec((GQA_TQ, gw), lambda b, g, i: (b * nq + i, g)),
        out_shape=jax.ShapeDtypeStruct((t, GQ_W), BF16),
        scratch_shapes=[
            pltpu.VMEM((GQA_GROUP, HEAD_DIM, GQA_TQ), BF16),
            pltpu.VMEM((seq_len, HEAD_DIM), BF16),
            pltpu.VMEM((seq_len // GQA_TK, HEAD_DIM + GQA_ONES_ROWS, GQA_TK), BF16),
            pltpu.VMEM((GQA_GROUP, 1, GQA_TQ), F32),
            pltpu.VMEM((GQA_GROUP, HEAD_DIM + GQA_ONES_ROWS, GQA_TQ), F32),
            pltpu.VMEM((GQA_GROUP, GQA_TK, GQA_TQ), F32),
            pltpu.VMEM((GQA_GROUP, GQA_TK, GQA_TQ), F32),
        ],
        compiler_params=pltpu.CompilerParams(
            dimension_semantics=("parallel", "parallel", "arbitrary"),
            vmem_limit_bytes=VMEM_LIMIT_BYTES),
    )(proj, proj, proj, proj, cos_t, sin_t, cos_t.T, sin_t.T)


def _tail_kernel(x_ref, oa_ref, ob_ref, ga_ref, gb_ref, wa_ref, wb_ref, wo_ref, y_ref):
    pa = jnp.dot(oa_ref[...], wa_ref[...], preferred_element_type=F32)
    pb = jnp.dot(ob_ref[...], wb_ref[...], preferred_element_type=F32)
    merged = ga_ref[...].astype(F32) * pa + gb_ref[...].astype(F32) * pb
    y_ref[...] = x_ref[...] + jnp.dot(merged.astype(BF16), wo_ref[...], preferred_element_type=F32)


def _tail(x2d, oa, ob, proj, wa_bf, wb_bf, wo_bf):
    t = x2d.shape[0]
    assert t % TAIL_TM == 0
    resident = dict(pipeline_mode=pl.Buffered(1))
    return pl.pallas_call(
        _tail_kernel,
        name="tail",
        grid=(t // TAIL_TM,),
        in_specs=[
            pl.BlockSpec((TAIL_TM, D_MODEL), lambda i: (i, 0)),
            pl.BlockSpec((TAIL_TM, NA_W), lambda i: (i, 0)),
            pl.BlockSpec((TAIL_TM, GQ_W), lambda i: (i, 0)),
            pl.BlockSpec((TAIL_TM, D_MODEL), lambda i: (i, _slab(OFF_GA) // D_MODEL)),
            pl.BlockSpec((TAIL_TM, D_MODEL), lambda i: (i, _slab(OFF_GB) // D_MODEL)),
            pl.BlockSpec((NA_W, D_MODEL), lambda i: (0, 0), **resident),
            pl.BlockSpec((GQ_W, D_MODEL), lambda i: (0, 0), **resident),
            pl.BlockSpec((D_MODEL, D_MODEL), lambda i: (0, 0), **resident),
        ],
        out_specs=pl.BlockSpec((TAIL_TM, D_MODEL), lambda i: (i, 0)),
        out_shape=jax.ShapeDtypeStruct((t, D_MODEL), F32),
        compiler_params=pltpu.CompilerParams(
            dimension_semantics=("parallel",),
            vmem_limit_bytes=VMEM_LIMIT_BYTES),
    )(x2d, oa, ob, proj, proj, wa_bf, wb_bf, wo_bf)


def _rope_tables(seq_len):
    pos = jnp.arange(seq_len)
    r = (pos // GRID_W).astype(F32)
    c = (pos % GRID_W).astype(F32)
    axis = HEAD_DIM // 2
    freqs = ROPE_THETA ** (-jnp.arange(0, axis, 2, dtype=F32) / axis)
    ang_r = r[:, None] * freqs[None, :]
    ang_c = c[:, None] * freqs[None, :]
    cos_t = jnp.concatenate([jnp.cos(ang_r), jnp.cos(ang_r), jnp.cos(ang_c), jnp.cos(ang_c)], axis=-1)
    sin_t = jnp.concatenate([-jnp.sin(ang_r), jnp.sin(ang_r), -jnp.sin(ang_c), jnp.sin(ang_c)], axis=-1)
    return cos_t, sin_t


def _layer(x, norm_g, w_in_bf, head_g, bias_tbl, wa_bf, wb_bf, gate_bias, wo_bf):
    batch, seq_len, _ = x.shape
    x2d = x.reshape(batch * seq_len, D_MODEL)
    cos_t, sin_t = _rope_tables(seq_len)
    proj = _proj(x2d, norm_g, w_in_bf, head_g, gate_bias)
    oa = _na(proj, batch, seq_len, bias_tbl)
    ob = _gqa(proj, batch, seq_len, cos_t, sin_t)
    y = _tail(x2d, oa, ob, proj, wa_bf, wb_bf, wo_bf)
    return y.reshape(batch, seq_len, D_MODEL)


def kernel(x_prompt, x_sample, norm_g, w_in, na_q_g, na_k_g, na_rpb, gq_q_g, gq_k_g,
           w_branch_a, w_branch_b, gate_bias, w_out):
    depth = norm_g.shape[0]
    y_prompt, y_sample = x_prompt, x_sample
    for l in range(depth):
        head_g = jnp.stack([na_q_g[l], na_k_g[l], gq_q_g[l], gq_k_g[l]]).astype(F32)
        params = (norm_g[l][None, :].astype(F32), w_in[l].astype(BF16), head_g,
                  _na_bias_table(na_rpb[l]), w_branch_a[l].astype(BF16), w_branch_b[l].astype(BF16),
                  gate_bias[l][None, :].astype(F32), w_out[l].astype(BF16))
        y_prompt = _layer(y_prompt, *params)
        y_sample = _layer(y_sample, *params)
    return (y_prompt, y_sample)
```

```python
import functools

import numpy as np
import jax
import jax.numpy as jnp
from jax import lax
from jax.experimental import pallas as pl
from jax.experimental.pallas import tpu as pltpu

F32 = jnp.float32
BF16 = jnp.bfloat16

D_MODEL = 2048
HEAD_DIM = 128
GRID_W = 64
NA_HEADS = 8
NA_KH = 8
NA_KW = 16
GQA_HEADS = 8
GQA_KV_HEADS = 2
GQA_GROUP = GQA_HEADS // GQA_KV_HEADS
ROPE_THETA = 10000.0
EPS = 1e-6
SCALE = HEAD_DIM ** -0.5

NA_W = NA_HEADS * HEAD_DIM
GQ_W = GQA_HEADS * HEAD_DIM
KV_W = GQA_KV_HEADS * HEAD_DIM
OFF_NA_Q = 0
OFF_NA_K = OFF_NA_Q + NA_W
OFF_NA_V = OFF_NA_K + NA_W
OFF_NA_Z = OFF_NA_V + NA_W
OFF_GQ_Q = OFF_NA_Z + NA_W
OFF_GQ_K = OFF_GQ_Q + GQ_W
OFF_GQ_V = OFF_GQ_K + KV_W
OFF_GQ_Z = OFF_GQ_V + KV_W
OFF_GA = OFF_GQ_Z + GQ_W
OFF_GB = OFF_GA + D_MODEL
IN_WIDTH = OFF_GB + D_MODEL


def _slab(off):
    return (off + 2 * D_MODEL) % IN_WIDTH

VMEM_LIMIT_BYTES = 56 * 1024 * 1024

PROJ_TM = 1024
PROJ_TN = 1536
PROJ_PAIR = 256
NA_QROWS = 4
NA_TQ = NA_QROWS * GRID_W
NA_WCHUNKS = 3
NA_WROWS = NA_WCHUNKS * NA_QROWS
NA_TK = NA_WROWS * GRID_W
NA_NB = 4
NA_ONES_ROWS = 16
GQA_TQ = 512
GQA_TK = 512
GQA_ONES_ROWS = 16
EXP2_SCALE = SCALE * float(np.log2(np.e))
TAIL_TM = 256
NEG_BIG = -1e30


def _sigmoid(x):
    return 0.5 * jnp.tanh(0.5 * x) + 0.5


def _proj_kernel(x_ref, ng_ref, w_ref, hg_ref, gb_ref, o_ref, h_scr):
    j = pl.program_id(1)

    @pl.when(j == 0)
    def _():
        x = x_ref[...]
        ms = jnp.mean(x * x, axis=-1, keepdims=True)
        h_scr[...] = (x * lax.rsqrt(ms + EPS) * ng_ref[...]).astype(BF16)

    def epilogue(col, a):
        kind, arg = _slab_head_kind(col)
        if kind == "gate":
            return _sigmoid(a + gb_ref[:, arg:arg + HEAD_DIM])
        if kind == "norm":
            ms = jnp.mean(a * a, axis=-1, keepdims=True)
            return a * lax.rsqrt(ms + EPS) * hg_ref[arg:arg + 1, :]
        if kind == "silu":
            return a * _sigmoid(a)
        return a

    for jj in range(IN_WIDTH // PROJ_TN):
        @pl.when(j == jj)
        def _(jj=jj):
            for pp in range(PROJ_TN // PROJ_PAIR):
                cols = slice(pp * PROJ_PAIR, (pp + 1) * PROJ_PAIR)
                acc = jnp.dot(h_scr[...], w_ref[:, cols], preferred_element_type=F32)
                for half in range(PROJ_PAIR // HEAD_DIM):
                    c = pp * PROJ_PAIR + half * HEAD_DIM
                    val = epilogue(jj * PROJ_TN + c, acc[:, half * HEAD_DIM:(half + 1) * HEAD_DIM])
                    o_ref[:, c:c + HEAD_DIM] = val.astype(o_ref.dtype)


def _slab_head_kind(col):
    groups = ((OFF_GA, 2 * D_MODEL, "gate", None), (OFF_NA_Q, NA_W, "norm", 0), (OFF_NA_K, NA_W, "norm", 1),
              (OFF_NA_V, NA_W, "copy", None), (OFF_NA_Z, NA_W, "silu", None), (OFF_GQ_Q, GQ_W, "norm", 2),
              (OFF_GQ_K, KV_W, "norm", 3), (OFF_GQ_V, KV_W, "copy", None), (OFF_GQ_Z, GQ_W, "silu", None))
    for off, width, kind, arg in groups:
        if _slab(off) <= col < _slab(off) + width:
            return kind, (col - _slab(off)) if kind == "gate" else arg
    raise ValueError(col)


def _proj(x2d, norm_g, w_slab_bf, head_g, gate_bias):
    t = x2d.shape[0]
    tm = PROJ_TM
    assert t % tm == 0 and IN_WIDTH % PROJ_TN == 0 and PROJ_TN % PROJ_PAIR == 0
    return pl.pallas_call(
        _proj_kernel,
        name="proj",
        grid=(t // tm, IN_WIDTH // PROJ_TN),
        in_specs=[
            pl.BlockSpec((tm, D_MODEL), lambda i, j: (i, 0)),
            pl.BlockSpec((1, D_MODEL), lambda i, j: (0, 0)),
            pl.BlockSpec((D_MODEL, PROJ_TN), lambda i, j: (0, j)),
            pl.BlockSpec((4, HEAD_DIM), lambda i, j: (0, 0)),
            pl.BlockSpec((1, 2 * D_MODEL), lambda i, j: (0, 0)),
        ],
        out_specs=pl.BlockSpec((tm, PROJ_TN), lambda i, j: (i, j)),
        out_shape=jax.ShapeDtypeStruct((t, IN_WIDTH), BF16),
        scratch_shapes=[pltpu.VMEM((tm, D_MODEL), BF16)],
        compiler_params=pltpu.CompilerParams(
            dimension_semantics=("parallel", "arbitrary"),
            vmem_limit_bytes=VMEM_LIMIT_BYTES),
    )(x2d, norm_g, w_slab_bf, head_g, gate_bias)


def _na_kernel(q_ref, k_ref, v_ref, z_ref, bias_ref, o_ref, vt_scr, s_scr, *, nblk):
    i = pl.program_id(2)

    @pl.when(i == 0)
    def _():
        def fill(c, carry):
            rows = pl.ds(pl.multiple_of(c * NA_TQ, NA_TQ), NA_TQ)
            vt_scr[c, 0:HEAD_DIM, :] = v_ref[rows, :].astype(F32).T.astype(BF16)
            vt_scr[c, HEAD_DIM:, :] = jnp.ones((NA_ONES_ROWS, NA_TQ), BF16)
            return carry
        lax.fori_loop(0, nblk, fill, 0)

    def first_chunk(nb):
        return jnp.clip(nb - 1, 0, nblk - NA_WCHUNKS)

    for jb in range(NA_NB):
        start = pl.multiple_of(first_chunk(i * NA_NB + jb) * NA_TQ, NA_TQ)
        k = k_ref[pl.ds(start, NA_TK), :]
        q = q_ref[jb * NA_TQ:(jb + 1) * NA_TQ, :]
        s_scr[jb] = lax.dot_general(k, q, (((1,), (1,)), ((), ())), preferred_element_type=F32)

    for jb in range(NA_NB):
        nb = i * NA_NB + jb
        chunk0 = first_chunk(nb)
        kind = jnp.where(nb == 0, 0, jnp.where(nb == nblk - 1, 2, 1))
        logits = EXP2_SCALE * s_scr[jb] + bias_ref[kind, 0]
        m = jnp.max(logits, axis=0, keepdims=True)
        p = jnp.exp2(logits - m).astype(BF16)
        pv = jnp.dot(vt_scr[chunk0], p[0:NA_TQ], preferred_element_type=F32)
        for w in range(1, NA_WCHUNKS):
            pv += jnp.dot(vt_scr[chunk0 + w], p[w * NA_TQ:(w + 1) * NA_TQ],
                          preferred_element_type=F32)
        o = (pv[0:HEAD_DIM] / pv[HEAD_DIM:HEAD_DIM + 1]).T
        rows = slice(jb * NA_TQ, (jb + 1) * NA_TQ)
        o_ref[rows, :] = (o * z_ref[rows, :].astype(F32)).astype(o_ref.dtype)


def _na_bias_table(rpb):
    c = np.arange(GRID_W)[:, None]
    kc = np.arange(GRID_W)[None, :]
    cs = np.clip(c - NA_KW // 2, 0, GRID_W - NA_KW)
    col_ok = (kc >= cs) & (kc < cs + NA_KW)
    dc = kc - c + NA_KW - 1
    onehot = ((dc.T[None] == np.arange(2 * NA_KW - 1)[:, None, None]) & col_ok.T[None]).astype(np.float32)
    t1 = jnp.einsum('hrd,dkc->hrkc', rpb.astype(F32), onehot, precision=lax.Precision.HIGHEST)
    block_off = (0, NA_KH // 2, NA_WROWS - NA_QROWS)
    pad = NA_QROWS
    t1 = jnp.pad(t1, ((0, 0), (pad, pad), (0, 0), (0, 0)))
    tbl = jnp.stack([
        jnp.stack([lax.slice_in_dim(t1, NA_KH - 1 - off - a + pad, NA_KH - 1 - off - a + pad + NA_WROWS,
                                    axis=1) for a in range(NA_QROWS)], axis=3)
        for off in block_off])
    a = np.arange(NA_QROWS)[None, :, None]
    jw = np.arange(NA_WROWS)[None, None, :]
    first_key = np.stack([np.zeros_like(a), a, np.full_like(a, NA_WROWS - NA_KH)])[:, 0]
    row_ok = (jw >= first_key) & (jw < first_key + NA_KH)
    ok = row_ok.transpose(0, 2, 1)[:, None, :, None, :, None] & col_ok.T[None, None, None, :, None, :]
    tbl = jnp.where(ok, tbl * float(np.log2(np.e)), NEG_BIG)
    return tbl.reshape(3, NA_HEADS, NA_TK, NA_TQ)


def _na(proj, batch, seq_len, bias_tbl):
    t = proj.shape[0]
    rows = seq_len // GRID_W
    nblk = rows // NA_QROWS
    assert rows % (NA_QROWS * NA_NB) == 0 and nblk >= NA_WCHUNKS
    nstep = nblk // NA_NB
    tq = NA_NB * NA_TQ
    cb = lambda off: _slab(off) // HEAD_DIM
    return pl.pallas_call(
        functools.partial(_na_kernel, nblk=nblk),
        name="na",
        grid=(NA_HEADS, batch, nstep),
        in_specs=[
            pl.BlockSpec((tq, HEAD_DIM), lambda h, b, i: (b * nstep + i, cb(OFF_NA_Q) + h)),
            pl.BlockSpec((seq_len, HEAD_DIM), lambda h, b, i: (b, cb(OFF_NA_K) + h)),
            pl.BlockSpec((seq_len, HEAD_DIM), lambda h, b, i: (b, cb(OFF_NA_V) + h)),
            pl.BlockSpec((tq, HEAD_DIM), lambda h, b, i: (b * nstep + i, cb(OFF_NA_Z) + h)),
            pl.BlockSpec((3, 1, NA_TK, NA_TQ), lambda h, b, i: (0, h, 0, 0)),
        ],
        out_specs=pl.BlockSpec((tq, HEAD_DIM), lambda h, b, i: (b * nstep + i, h)),
        out_shape=jax.ShapeDtypeStruct((t, NA_W), BF16),
        scratch_shapes=[
            pltpu.VMEM((nblk, HEAD_DIM + NA_ONES_ROWS, NA_TQ), BF16),
            pltpu.VMEM((NA_NB, NA_TK, NA_TQ), F32),
        ],
        compiler_params=pltpu.CompilerParams(
            dimension_semantics=("parallel", "parallel", "arbitrary"),
            vmem_limit_bytes=VMEM_LIMIT_BYTES),
    )(proj, proj, proj, proj, bias_tbl)


def _gqa_kernel(q_ref, k_ref, v_ref, z_ref, cos_ref, sin_ref, cost_ref, sint_ref, o_ref,
                qt_scr, k_scr, vt_scr, m_scr, acc_scr, sa_scr, sb_scr, *, nk):
    i = pl.program_id(2)
    quarter = HEAD_DIM // 4

    @pl.when(i == 0)
    def _():
        def fill(c, carry):
            start = pl.multiple_of(c * GQA_TK, GQA_TK)
            rows = pl.ds(start, GQA_TK)
            v = v_ref[rows, :].astype(F32)
            vt_scr[c, 0:HEAD_DIM, :] = v.T.astype(BF16)
            vt_scr[c, HEAD_DIM:, :] = jnp.ones((GQA_ONES_ROWS, GQA_TK), BF16)
            kk = k_ref[rows, :].astype(F32)
            lane = lax.broadcasted_iota(jnp.int32, kk.shape, 1)
            partner = jnp.where((lane % (2 * quarter)) < quarter,
                                pltpu.roll(kk, HEAD_DIM - quarter, 1),
                                pltpu.roll(kk, quarter, 1))
            k_scr[rows, :] = (kk * cos_ref[rows, :] + partner * sin_ref[rows, :]).astype(BF16)
            return carry
        lax.fori_loop(0, nk, fill, 0)

    for g in range(GQA_GROUP):
        qt = q_ref[:, g * HEAD_DIM:(g + 1) * HEAD_DIM].astype(F32).T
        partner = jnp.concatenate([qt[quarter:2 * quarter], qt[0:quarter],
                                   qt[3 * quarter:], qt[2 * quarter:3 * quarter]], axis=0)
        qt_scr[g] = (qt * cost_ref[...] + partner * sint_ref[...]).astype(BF16)
    m_scr[...] = jnp.full(m_scr.shape, -jnp.inf, F32)
    acc_scr[...] = jnp.zeros(acc_scr.shape, F32)

    def scores(c, s_buf):
        start = pl.multiple_of(c * GQA_TK, GQA_TK)
        k = k_scr[pl.ds(start, GQA_TK), :]
        for g in range(GQA_GROUP):
            s_buf[g] = jnp.dot(k, qt_scr[g], preferred_element_type=F32)

    def accumulate(c, s_buf):
        vt = vt_scr[c]
        for g in range(GQA_GROUP):
            s = s_buf[g]
            m_old = m_scr[g]
            m_new = jnp.maximum(m_old, jnp.max(s, axis=0, keepdims=True))
            alpha = jnp.exp2(EXP2_SCALE * (m_old - m_new))
            p = jnp.exp2(EXP2_SCALE * (s - m_new)).astype(BF16)
            acc_scr[g] = alpha * acc_scr[g] + jnp.dot(vt, p, preferred_element_type=F32)
            m_scr[g] = m_new

    scores(0, sa_scr)

    def pair(j, carry):
        scores(2 * j + 1, sb_scr)
        accumulate(2 * j, sa_scr)
        scores(2 * j + 2, sa_scr)
        accumulate(2 * j + 1, sb_scr)
        return carry

    lax.fori_loop(0, nk // 2 - 1, pair, 0)
    scores(nk - 1, sb_scr)
    accumulate(nk - 2, sa_scr)
    accumulate(nk - 1, sb_scr)
    for g in range(GQA_GROUP):
        cols = slice(g * HEAD_DIM, (g + 1) * HEAD_DIM)
        acc = acc_scr[g]
        o = (acc[0:HEAD_DIM] / acc[HEAD_DIM:HEAD_DIM + 1]).T
        o_ref[:, cols] = (o * z_ref[:, cols].astype(F32)).astype(o_ref.dtype)


def _gqa(proj, batch, seq_len, cos_t, sin_t, cos_tt, sin_tt):
    t = proj.shape[0]
    assert seq_len % GQA_TQ == 0 and seq_len % (2 * GQA_TK) == 0
    nq = seq_len // GQA_TQ
    gw = GQA_GROUP * HEAD_DIM
    return pl.pallas_call(
        functools.partial(_gqa_kernel, nk=seq_len // GQA_TK),
        name="gqa",
        grid=(batch, GQA_KV_HEADS, nq),
        in_specs=[
            pl.BlockSpec((GQA_TQ, gw), lambda b, g, i: (b * nq + i, _slab(OFF_GQ_Q) // gw + g)),
            pl.BlockSpec((seq_len, HEAD_DIM), lambda b, g, i: (b, _slab(OFF_GQ_K) // HEAD_DIM + g)),
            pl.BlockSpec((seq_len, HEAD_DIM), lambda b, g, i: (b, _slab(OFF_GQ_V) // HEAD_DIM + g)),
            pl.BlockSpec((GQA_TQ, gw), lambda b, g, i: (b * nq + i, _slab(OFF_GQ_Z) // gw + g)),
            pl.BlockSpec((seq_len, HEAD_DIM), lambda b, g, i: (0, 0), pipeline_mode=pl.Buffered(1)),
            pl.BlockSpec((seq_len, HEAD_DIM), lambda b, g, i: (0, 0), pipeline_mode=pl.Buffered(1)),
            pl.BlockSpec((HEAD_DIM, GQA_TQ), lambda b, g, i: (0, i)),
            pl.BlockSpec((HEAD_DIM, GQA_TQ), lambda b, g, i: (0, i)),
        ],
        out_specs=pl.BlockSpec((GQA_TQ, gw), lambda b, g, i: (b * nq + i, g)),
        out_shape=jax.ShapeDtypeStruct((t, GQ_W), BF16),
        scratch_shapes=[
            pltpu.VMEM((GQA_GROUP, HEAD_DIM, GQA_TQ), BF16),
            pltpu.VMEM((seq_len, HEAD_DIM), BF16),
            pltpu.VMEM((seq_len // GQA_TK, HEAD_DIM + GQA_ONES_ROWS, GQA_TK), BF16),
            pltpu.VMEM((GQA_GROUP, 1, GQA_TQ), F32),
            pltpu.VMEM((GQA_GROUP, HEAD_DIM + GQA_ONES_ROWS, GQA_TQ), F32),
            pltpu.VMEM((GQA_GROUP, GQA_TK, GQA_TQ), F32),
            pltpu.VMEM((GQA_GROUP, GQA_TK, GQA_TQ), F32),
        ],
        compiler_params=pltpu.CompilerParams(
            dimension_semantics=("parallel", "parallel", "arbitrary"),
            vmem_limit_bytes=VMEM_LIMIT_BYTES),
    )(proj, proj, proj, proj, cos_t, sin_t, cos_tt, sin_tt)


def _tail_kernel(x_ref, oa_ref, ob_ref, ga_ref, gb_ref, wa_ref, wb_ref, wo_ref, y_ref):
    pa = jnp.dot(oa_ref[...], wa_ref[...], preferred_element_type=F32)
    pb = jnp.dot(ob_ref[...], wb_ref[...], preferred_element_type=F32)
    merged = ga_ref[...].astype(F32) * pa + gb_ref[...].astype(F32) * pb
    y_ref[...] = x_ref[...] + jnp.dot(merged.astype(BF16), wo_ref[...], preferred_element_type=F32)


def _tail(x2d, oa, ob, proj, wa_bf, wb_bf, wo_bf):
    t = x2d.shape[0]
    assert t % TAIL_TM == 0
    resident = dict(pipeline_mode=pl.Buffered(1))
    return pl.pallas_call(
        _tail_kernel,
        name="tail",
        grid=(t // TAIL_TM,),
        in_specs=[
            pl.BlockSpec((TAIL_TM, D_MODEL), lambda i: (i, 0)),
            pl.BlockSpec((TAIL_TM, NA_W), lambda i: (i, 0)),
            pl.BlockSpec((TAIL_TM, GQ_W), lambda i: (i, 0)),
            pl.BlockSpec((TAIL_TM, D_MODEL), lambda i: (i, _slab(OFF_GA) // D_MODEL)),
            pl.BlockSpec((TAIL_TM, D_MODEL), lambda i: (i, _slab(OFF_GB) // D_MODEL)),
            pl.BlockSpec((NA_W, D_MODEL), lambda i: (0, 0), **resident),
            pl.BlockSpec((GQ_W, D_MODEL), lambda i: (0, 0), **resident),
            pl.BlockSpec((D_MODEL, D_MODEL), lambda i: (0, 0), **resident),
        ],
        out_specs=pl.BlockSpec((TAIL_TM, D_MODEL), lambda i: (i, 0)),
        out_shape=jax.ShapeDtypeStruct((t, D_MODEL), F32),
        compiler_params=pltpu.CompilerParams(
            dimension_semantics=("parallel",),
            vmem_limit_bytes=VMEM_LIMIT_BYTES),
    )(x2d, oa, ob, proj, proj, wa_bf, wb_bf, wo_bf)


def _rope_tables(seq_len):
    rows = seq_len // GRID_W
    axis = HEAD_DIM // 2
    freqs = ROPE_THETA ** (-jnp.arange(0, axis, 2, dtype=F32) / axis)
    ang_r = jnp.arange(rows).astype(F32)[:, None] * freqs[None, :]
    ang_c = jnp.arange(GRID_W).astype(F32)[:, None] * freqs[None, :]
    cos_r, sin_r, cos_c, sin_c = jnp.cos(ang_r), jnp.sin(ang_r), jnp.cos(ang_c), jnp.sin(ang_c)
    cos_small = (jnp.concatenate([cos_r, cos_r], axis=-1), jnp.concatenate([cos_c, cos_c], axis=-1))
    sin_small = (jnp.concatenate([-sin_r, sin_r], axis=-1), jnp.concatenate([-sin_c, sin_c], axis=-1))

    def expand(row_part, col_part):
        by_token = jnp.concatenate([jnp.repeat(row_part, GRID_W, axis=0), jnp.tile(col_part, (rows, 1))], axis=-1)
        by_dim = jnp.concatenate([jnp.repeat(row_part.T, GRID_W, axis=1), jnp.tile(col_part.T, (1, rows))], axis=0)
        return by_token, by_dim

    cos_t, cos_tt = expand(*cos_small)
    sin_t, sin_tt = expand(*sin_small)
    return cos_t, sin_t, cos_tt, sin_tt


def _layer(x, norm_g, w_in_bf, head_g, bias_tbl, wa_bf, wb_bf, gate_bias, wo_bf):
    batch, seq_len, _ = x.shape
    x2d = x.reshape(batch * seq_len, D_MODEL)
    proj = _proj(x2d, norm_g, w_in_bf, head_g, gate_bias)
    oa = _na(proj, batch, seq_len, bias_tbl)
    ob = _gqa(proj, batch, seq_len, *_rope_tables(seq_len))
    y = _tail(x2d, oa, ob, proj, wa_bf, wb_bf, wo_bf)
    return y.reshape(batch, seq_len, D_MODEL)


def kernel(x_prompt, x_sample, norm_g, w_in, na_q_g, na_k_g, na_rpb, gq_q_g, gq_k_g,
           w_branch_a, w_branch_b, gate_bias, w_out):
    depth = norm_g.shape[0]
    y_prompt, y_sample = x_prompt, x_sample
    for l in range(depth):
        head_g = jnp.stack([na_q_g[l], na_k_g[l], gq_q_g[l], gq_k_g[l]]).astype(F32)
        w_slab = jnp.concatenate([w_in[l][:, OFF_GA:], w_in[l][:, :OFF_GA]], axis=1).astype(BF16)
        params = (norm_g[l][None, :].astype(F32), w_slab, head_g,
                  _na_bias_table(na_rpb[l]), w_branch_a[l].astype(BF16), w_branch_b[l].astype(BF16),
                  gate_bias[l][None, :].astype(F32), w_out[l].astype(BF16))
        y_prompt = _layer(y_prompt, *params)
        y_sample = _layer(y_sample, *params)
    return (y_prompt, y_sample)
```

```python
import functools

import numpy as np
import jax
import jax.numpy as jnp
from jax import lax
from jax.experimental import pallas as pl
from jax.experimental.pallas import tpu as pltpu

F32 = jnp.float32
BF16 = jnp.bfloat16

D_MODEL = 2048
HEAD_DIM = 128
GRID_W = 64
NA_HEADS = 8
NA_KH = 8
NA_KW = 16
GQA_HEADS = 8
GQA_KV_HEADS = 2
GQA_GROUP = GQA_HEADS // GQA_KV_HEADS
ROPE_THETA = 10000.0
EPS = 1e-6
SCALE = HEAD_DIM ** -0.5

NA_W = NA_HEADS * HEAD_DIM
GQ_W = GQA_HEADS * HEAD_DIM
KV_W = GQA_KV_HEADS * HEAD_DIM
OFF_NA_Q = 0
OFF_NA_K = OFF_NA_Q + NA_W
OFF_NA_V = OFF_NA_K + NA_W
OFF_NA_Z = OFF_NA_V + NA_W
OFF_GQ_Q = OFF_NA_Z + NA_W
OFF_GQ_K = OFF_GQ_Q + GQ_W
OFF_GQ_V = OFF_GQ_K + KV_W
OFF_GQ_Z = OFF_GQ_V + KV_W
OFF_GA = OFF_GQ_Z + GQ_W
OFF_GB = OFF_GA + D_MODEL
IN_WIDTH = OFF_GB + D_MODEL


def _slab(off):
    return (off + 2 * D_MODEL) % IN_WIDTH

VMEM_LIMIT_BYTES = 56 * 1024 * 1024

PROJ_TM = 1024
PROJ_TN = 1536
PROJ_PAIR = 256
NA_QROWS = 4
NA_TQ = NA_QROWS * GRID_W
NA_WCHUNKS = 3
NA_WROWS = NA_WCHUNKS * NA_QROWS
NA_TK = NA_WROWS * GRID_W
NA_NB = 8
NA_ONES_ROWS = 16
GQA_TQ = 512
GQA_TK = 512
GQA_ONES_ROWS = 16
EXP2_SCALE = SCALE * float(np.log2(np.e))
TAIL_TM = 256
NEG_BIG = -1e30


def _sigmoid(x):
    return 0.5 * jnp.tanh(0.5 * x) + 0.5


def _proj_kernel(x_ref, ng_ref, w_ref, hg_ref, gb_ref, o_ref, h_scr):
    j = pl.program_id(1)

    @pl.when(j == 0)
    def _():
        x = x_ref[...]
        ms = jnp.mean(x * x, axis=-1, keepdims=True)
        h_scr[...] = (x * lax.rsqrt(ms + EPS) * ng_ref[...]).astype(BF16)

    def epilogue(col, a):
        kind, arg = _slab_head_kind(col)
        if kind == "gate":
            return _sigmoid(a + gb_ref[:, arg:arg + HEAD_DIM])
        if kind == "norm":
            ms = jnp.mean(a * a, axis=-1, keepdims=True)
            return a * lax.rsqrt(ms + EPS) * hg_ref[arg:arg + 1, :]
        if kind == "silu":
            return a * _sigmoid(a)
        return a

    for jj in range(IN_WIDTH // PROJ_TN):
        @pl.when(j == jj)
        def _(jj=jj):
            for pp in range(PROJ_TN // PROJ_PAIR):
                cols = slice(pp * PROJ_PAIR, (pp + 1) * PROJ_PAIR)
                acc = jnp.dot(h_scr[...], w_ref[:, cols], preferred_element_type=F32)
                for half in range(PROJ_PAIR // HEAD_DIM):
                    c = pp * PROJ_PAIR + half * HEAD_DIM
                    val = epilogue(jj * PROJ_TN + c, acc[:, half * HEAD_DIM:(half + 1) * HEAD_DIM])
                    o_ref[:, c:c + HEAD_DIM] = val.astype(o_ref.dtype)


def _slab_head_kind(col):
    groups = ((OFF_GA, 2 * D_MODEL, "gate", None), (OFF_NA_Q, NA_W, "norm", 0), (OFF_NA_K, NA_W, "norm", 1),
              (OFF_NA_V, NA_W, "copy", None), (OFF_NA_Z, NA_W, "silu", None), (OFF_GQ_Q, GQ_W, "norm", 2),
              (OFF_GQ_K, KV_W, "norm", 3), (OFF_GQ_V, KV_W, "copy", None), (OFF_GQ_Z, GQ_W, "silu", None))
    for off, width, kind, arg in groups:
        if _slab(off) <= col < _slab(off) + width:
            return kind, (col - _slab(off)) if kind == "gate" else arg
    raise ValueError(col)


def _proj(x2d, norm_g, w_slab_bf, head_g, gate_bias):
    t = x2d.shape[0]
    tm = PROJ_TM
    assert t % tm == 0 and IN_WIDTH % PROJ_TN == 0 and PROJ_TN % PROJ_PAIR == 0
    return pl.pallas_call(
        _proj_kernel,
        name="proj",
        grid=(t // tm, IN_WIDTH // PROJ_TN),
        in_specs=[
            pl.BlockSpec((tm, D_MODEL), lambda i, j: (i, 0)),
            pl.BlockSpec((1, D_MODEL), lambda i, j: (0, 0)),
            pl.BlockSpec((D_MODEL, PROJ_TN), lambda i, j: (0, j)),
            pl.BlockSpec((4, HEAD_DIM), lambda i, j: (0, 0)),
            pl.BlockSpec((1, 2 * D_MODEL), lambda i, j: (0, 0)),
        ],
        out_specs=pl.BlockSpec((tm, PROJ_TN), lambda i, j: (i, j)),
        out_shape=jax.ShapeDtypeStruct((t, IN_WIDTH), BF16),
        scratch_shapes=[pltpu.VMEM((tm, D_MODEL), BF16)],
        compiler_params=pltpu.CompilerParams(
            dimension_semantics=("parallel", "arbitrary"),
            vmem_limit_bytes=VMEM_LIMIT_BYTES),
    )(x2d, norm_g, w_slab_bf, head_g, gate_bias)


def _na_kernel(q_ref, k_ref, v_ref, z_ref, bias_ref, o_ref, vt_scr, s_scr, *, nblk):
    i = pl.program_id(2)

    @pl.when(i == 0)
    def _():
        def fill(c, carry):
            rows = pl.ds(pl.multiple_of(c * NA_TQ, NA_TQ), NA_TQ)
            vt_scr[c, 0:HEAD_DIM, :] = v_ref[rows, :].astype(F32).T.astype(BF16)
            vt_scr[c, HEAD_DIM:, :] = jnp.ones((NA_ONES_ROWS, NA_TQ), BF16)
            return carry
        lax.fori_loop(0, nblk, fill, 0)

    def first_chunk(nb):
        return jnp.clip(nb - 1, 0, nblk - NA_WCHUNKS)

    for jb in range(NA_NB):
        start = pl.multiple_of(first_chunk(i * NA_NB + jb) * NA_TQ, NA_TQ)
        k = k_ref[pl.ds(start, NA_TK), :]
        q = q_ref[jb * NA_TQ:(jb + 1) * NA_TQ, :]
        s_scr[jb] = lax.dot_general(k, q, (((1,), (1,)), ((), ())), preferred_element_type=F32)

    for jb in range(NA_NB):
        nb = i * NA_NB + jb
        chunk0 = first_chunk(nb)
        kind = jnp.where(nb == 0, 0, jnp.where(nb == nblk - 1, 2, 1))
        logits = EXP2_SCALE * s_scr[jb] + bias_ref[kind, 0]
        m = jnp.max(logits, axis=0, keepdims=True)
        p = jnp.exp2(logits - m).astype(BF16)
        pv = jnp.dot(vt_scr[chunk0], p[0:NA_TQ], preferred_element_type=F32)
        for w in range(1, NA_WCHUNKS):
            pv += jnp.dot(vt_scr[chunk0 + w], p[w * NA_TQ:(w + 1) * NA_TQ],
                          preferred_element_type=F32)
        o = (pv[0:HEAD_DIM] / pv[HEAD_DIM:HEAD_DIM + 1]).T
        rows = slice(jb * NA_TQ, (jb + 1) * NA_TQ)
        o_ref[rows, :] = (o * z_ref[rows, :].astype(F32)).astype(o_ref.dtype)


def _na_bias_table(rpb):
    c = np.arange(GRID_W)[:, None]
    kc = np.arange(GRID_W)[None, :]
    cs = np.clip(c - NA_KW // 2, 0, GRID_W - NA_KW)
    col_ok = (kc >= cs) & (kc < cs + NA_KW)
    dc = kc - c + NA_KW - 1
    onehot = ((dc.T[None] == np.arange(2 * NA_KW - 1)[:, None, None]) & col_ok.T[None]).astype(np.float32)
    t1 = jnp.einsum('hrd,dkc->hrkc', rpb.astype(F32), onehot, precision=lax.Precision.HIGHEST)
    block_off = (0, NA_KH // 2, NA_WROWS - NA_QROWS)
    pad = NA_QROWS
    t1 = jnp.pad(t1, ((0, 0), (pad, pad), (0, 0), (0, 0)))
    tbl = jnp.stack([
        jnp.stack([lax.slice_in_dim(t1, NA_KH - 1 - off - a + pad, NA_KH - 1 - off - a + pad + NA_WROWS,
                                    axis=1) for a in range(NA_QROWS)], axis=3)
        for off in block_off])
    a = np.arange(NA_QROWS)[None, :, None]
    jw = np.arange(NA_WROWS)[None, None, :]
    first_key = np.stack([np.zeros_like(a), a, np.full_like(a, NA_WROWS - NA_KH)])[:, 0]
    row_ok = (jw >= first_key) & (jw < first_key + NA_KH)
    ok = row_ok.transpose(0, 2, 1)[:, None, :, None, :, None] & col_ok.T[None, None, None, :, None, :]
    tbl = jnp.where(ok, tbl * float(np.log2(np.e)), NEG_BIG)
    return tbl.reshape(3, NA_HEADS, NA_TK, NA_TQ)


def _na(proj, batch, seq_len, bias_tbl):
    t = proj.shape[0]
    rows = seq_len // GRID_W
    nblk = rows // NA_QROWS
    assert rows % (NA_QROWS * NA_NB) == 0 and nblk >= NA_WCHUNKS
    nstep = nblk // NA_NB
    tq = NA_NB * NA_TQ
    cb = lambda off: _slab(off) // HEAD_DIM
    return pl.pallas_call(
        functools.partial(_na_kernel, nblk=nblk),
        name="na",
        grid=(NA_HEADS, batch, nstep),
        in_specs=[
            pl.BlockSpec((tq, HEAD_DIM), lambda h, b, i: (b * nstep + i, cb(OFF_NA_Q) + h)),
            pl.BlockSpec((seq_len, HEAD_DIM), lambda h, b, i: (b, cb(OFF_NA_K) + h)),
            pl.BlockSpec((seq_len, HEAD_DIM), lambda h, b, i: (b, cb(OFF_NA_V) + h)),
            pl.BlockSpec((tq, HEAD_DIM), lambda h, b, i: (b * nstep + i, cb(OFF_NA_Z) + h)),
            pl.BlockSpec((3, 1, NA_TK, NA_TQ), lambda h, b, i: (0, h, 0, 0)),
        ],
        out_specs=pl.BlockSpec((tq, HEAD_DIM), lambda h, b, i: (b * nstep + i, h)),
        out_shape=jax.ShapeDtypeStruct((t, NA_W), BF16),
        scratch_shapes=[
            pltpu.VMEM((nblk, HEAD_DIM + NA_ONES_ROWS, NA_TQ), BF16),
            pltpu.VMEM((NA_NB, NA_TK, NA_TQ), F32),
        ],
        compiler_params=pltpu.CompilerParams(
            dimension_semantics=("parallel", "parallel", "arbitrary"),
            vmem_limit_bytes=VMEM_LIMIT_BYTES),
    )(proj, proj, proj, proj, bias_tbl)


def _gqa_kernel(q_ref, k_ref, v_ref, z_ref, cos_ref, sin_ref, cost_ref, sint_ref, o_ref,
                qt_scr, k_scr, vt_scr, m_scr, acc_scr, sa_scr, sb_scr, *, nk):
    i = pl.program_id(2)
    quarter = HEAD_DIM // 4

    @pl.when(i == 0)
    def _():
        def fill(c, carry):
            start = pl.multiple_of(c * GQA_TK, GQA_TK)
            rows = pl.ds(start, GQA_TK)
            v = v_ref[rows, :].astype(F32)
            vt_scr[c, 0:HEAD_DIM, :] = v.T.astype(BF16)
            vt_scr[c, HEAD_DIM:, :] = jnp.ones((GQA_ONES_ROWS, GQA_TK), BF16)
            kk = k_ref[rows, :].astype(F32)
            lane = lax.broadcasted_iota(jnp.int32, kk.shape, 1)
            partner = jnp.where((lane % (2 * quarter)) < quarter,
                                pltpu.roll(kk, HEAD_DIM - quarter, 1),
                                pltpu.roll(kk, quarter, 1))
            k_scr[rows, :] = (kk * cos_ref[rows, :] + partner * sin_ref[rows, :]).astype(BF16)
            return carry
        lax.fori_loop(0, nk, fill, 0)

    for g in range(GQA_GROUP):
        qt = q_ref[:, g * HEAD_DIM:(g + 1) * HEAD_DIM].astype(F32).T
        partner = jnp.concatenate([qt[quarter:2 * quarter], qt[0:quarter],
                                   qt[3 * quarter:], qt[2 * quarter:3 * quarter]], axis=0)
        qt_scr[g] = (qt * cost_ref[...] + partner * sint_ref[...]).astype(BF16)
    m_scr[...] = jnp.full(m_scr.shape, -jnp.inf, F32)
    acc_scr[...] = jnp.zeros(acc_scr.shape, F32)

    def scores(c, s_buf):
        start = pl.multiple_of(c * GQA_TK, GQA_TK)
        k = k_scr[pl.ds(start, GQA_TK), :]
        for g in range(GQA_GROUP):
            s_buf[g] = jnp.dot(k, qt_scr[g], preferred_element_type=F32)

    def accumulate(c, s_buf):
        vt = vt_scr[c]
        for g in range(GQA_GROUP):
            s = s_buf[g]
            m_old = m_scr[g]
            m_new = jnp.maximum(m_old, jnp.max(s, axis=0, keepdims=True))
            alpha = jnp.exp2(EXP2_SCALE * (m_old - m_new))
            p = jnp.exp2(EXP2_SCALE * (s - m_new)).astype(BF16)
            acc_scr[g] = alpha * acc_scr[g] + jnp.dot(vt, p, preferred_element_type=F32)
            m_scr[g] = m_new

    scores(0, sa_scr)

    def pair(j, carry):
        scores(2 * j + 1, sb_scr)
        accumulate(2 * j, sa_scr)
        scores(2 * j + 2, sa_scr)
        accumulate(2 * j + 1, sb_scr)
        return carry

    lax.fori_loop(0, nk // 2 - 1, pair, 0)
    scores(nk - 1, sb_scr)
    accumulate(nk - 2, sa_scr)
    accumulate(nk - 1, sb_scr)
    for g in range(GQA_GROUP):
        cols = slice(g * HEAD_DIM, (g + 1) * HEAD_DIM)
        acc = acc_scr[g]
        o = (acc[0:HEAD_DIM] / acc[HEAD_DIM:HEAD_DIM + 1]).T
        o_ref[:, cols] = (o * z_ref[:, cols].astype(F32)).astype(o_ref.dtype)


def _gqa(proj, batch, seq_len, cos_t, sin_t, cos_tt, sin_tt):
    t = proj.shape[0]
    assert seq_len % GQA_TQ == 0 and seq_len % (2 * GQA_TK) == 0
    nq = seq_len // GQA_TQ
    gw = GQA_GROUP * HEAD_DIM
    return pl.pallas_call(
        functools.partial(_gqa_kernel, nk=seq_len // GQA_TK),
        name="gqa",
        grid=(batch, GQA_KV_HEADS, nq),
        in_specs=[
            pl.BlockSpec((GQA_TQ, gw), lambda b, g, i: (b * nq + i, _slab(OFF_GQ_Q) // gw + g)),
            pl.BlockSpec((seq_len, HEAD_DIM), lambda b, g, i: (b, _slab(OFF_GQ_K) // HEAD_DIM + g)),
            pl.BlockSpec((seq_len, HEAD_DIM), lambda b, g, i: (b, _slab(OFF_GQ_V) // HEAD_DIM + g)),
            pl.BlockSpec((GQA_TQ, gw), lambda b, g, i: (b * nq + i, _slab(OFF_GQ_Z) // gw + g)),
            pl.BlockSpec((seq_len, HEAD_DIM), lambda b, g, i: (0, 0), pipeline_mode=pl.Buffered(1)),
            pl.BlockSpec((seq_len, HEAD_DIM), lambda b, g, i: (0, 0), pipeline_mode=pl.Buffered(1)),
            pl.BlockSpec((HEAD_DIM, GQA_TQ), lambda b, g, i: (0, i)),
            pl.BlockSpec((HEAD_DIM, GQA_TQ), lambda b, g, i: (0, i)),
        ],
        out_specs=pl.BlockSpec((GQA_TQ, gw), lambda b, g, i: (b * nq + i, g)),
        out_shape=jax.ShapeDtypeStruct((t, GQ_W), BF16),
        scratch_shapes=[
            pltpu.VMEM((GQA_GROUP, HEAD_DIM, GQA_TQ), BF16),
            pltpu.VMEM((seq_len, HEAD_DIM), BF16),
            pltpu.VMEM((seq_len // GQA_TK, HEAD_DIM + GQA_ONES_ROWS, GQA_TK), BF16),
            pltpu.VMEM((GQA_GROUP, 1, GQA_TQ), F32),
            pltpu.VMEM((GQA_GROUP, HEAD_DIM + GQA_ONES_ROWS, GQA_TQ), F32),
            pltpu.VMEM((GQA_GROUP, GQA_TK, GQA_TQ), F32),
            pltpu.VMEM((GQA_GROUP, GQA_TK, GQA_TQ), F32),
        ],
        compiler_params=pltpu.CompilerParams(
            dimension_semantics=("parallel", "parallel", "arbitrary"),
            vmem_limit_bytes=VMEM_LIMIT_BYTES),
    )(proj, proj, proj, proj, cos_t, sin_t, cos_tt, sin_tt)


def _tail_kernel(x_ref, oa_ref, ob_ref, ga_ref, gb_ref, wa_ref, wb_ref, wo_ref, y_ref):
    pa = jnp.dot(oa_ref[...], wa_ref[...], preferred_element_type=F32)
    pb = jnp.dot(ob_ref[...], wb_ref[...], preferred_element_type=F32)
    merged = ga_ref[...].astype(F32) * pa + gb_ref[...].astype(F32) * pb
    y_ref[...] = x_ref[...] + jnp.dot(merged.astype(BF16), wo_ref[...], preferred_element_type=F32)


def _tail(x2d, oa, ob, proj, wa_bf, wb_bf, wo_bf):
    t = x2d.shape[0]
    assert t % TAIL_TM == 0
    resident = dict(pipeline_mode=pl.Buffered(1))
    return pl.pallas_call(
        _tail_kernel,
        name="tail",
        grid=(t // TAIL_TM,),
        in_specs=[
            pl.BlockSpec((TAIL_TM, D_MODEL), lambda i: (i, 0)),
            pl.BlockSpec((TAIL_TM, NA_W), lambda i: (i, 0)),
            pl.BlockSpec((TAIL_TM, GQ_W), lambda i: (i, 0)),
            pl.BlockSpec((TAIL_TM, D_MODEL), lambda i: (i, _slab(OFF_GA) // D_MODEL)),
            pl.BlockSpec((TAIL_TM, D_MODEL), lambda i: (i, _slab(OFF_GB) // D_MODEL)),
            pl.BlockSpec((NA_W, D_MODEL), lambda i: (0, 0), **resident),
            pl.BlockSpec((GQ_W, D_MODEL), lambda i: (0, 0), **resident),
            pl.BlockSpec((D_MODEL, D_MODEL), lambda i: (0, 0), **resident),
        ],
        out_specs=pl.BlockSpec((TAIL_TM, D_MODEL), lambda i: (i, 0)),
        out_shape=jax.ShapeDtypeStruct((t, D_MODEL), F32),
        compiler_params=pltpu.CompilerParams(
            dimension_semantics=("parallel",),
            vmem_limit_bytes=VMEM_LIMIT_BYTES),
    )(x2d, oa, ob, proj, proj, wa_bf, wb_bf, wo_bf)


def _rope_tables(seq_len):
    pos = np.arange(seq_len)
    axis = HEAD_DIM // 2
    freqs = ROPE_THETA ** (-np.arange(0, axis, 2, dtype=np.float64) / axis)
    ang_r = (pos // GRID_W)[:, None] * freqs[None, :]
    ang_c = (pos % GRID_W)[:, None] * freqs[None, :]
    cos_t = np.concatenate([np.cos(ang_r), np.cos(ang_r), np.cos(ang_c), np.cos(ang_c)], axis=-1)
    sin_t = np.concatenate([-np.sin(ang_r), np.sin(ang_r), -np.sin(ang_c), np.sin(ang_c)], axis=-1)
    tables = (cos_t, sin_t, cos_t.T, sin_t.T)
    return tuple(jnp.asarray(np.ascontiguousarray(tbl), dtype=F32) for tbl in tables)


def _layer(x, norm_g, w_in_bf, head_g, bias_tbl, wa_bf, wb_bf, gate_bias, wo_bf):
    batch, seq_len, _ = x.shape
    x2d = x.reshape(batch * seq_len, D_MODEL)
    proj = _proj(x2d, norm_g, w_in_bf, head_g, gate_bias)
    oa = _na(proj, batch, seq_len, bias_tbl)
    ob = _gqa(proj, batch, seq_len, *_rope_tables(seq_len))
    y = _tail(x2d, oa, ob, proj, wa_bf, wb_bf, wo_bf)
    return y.reshape(batch, seq_len, D_MODEL)


def kernel(x_prompt, x_sample, norm_g, w_in, na_q_g, na_k_g, na_rpb, gq_q_g, gq_k_g,
           w_branch_a, w_branch_b, gate_bias, w_out):
    depth = norm_g.shape[0]
    y_prompt, y_sample = x_prompt, x_sample
    for l in range(depth):
        head_g = jnp.stack([na_q_g[l], na_k_g[l], gq_q_g[l], gq_k_g[l]]).astype(F32)
        w_slab = jnp.concatenate([w_in[l][:, OFF_GA:], w_in[l][:, :OFF_GA]], axis=1).astype(BF16)
        params = (norm_g[l][None, :].astype(F32), w_slab, head_g,
                  _na_bias_table(na_rpb[l]), w_branch_a[l].astype(BF16), w_branch_b[l].astype(BF16),
                  gate_bias[l][None, :].astype(F32), w_out[l].astype(BF16))
        y_prompt = _layer(y_prompt, *params)
        y_sample = _layer(y_sample, *params)
    return (y_prompt, y_sample)
```

```python
import functools

import numpy as np
import jax
import jax.numpy as jnp
from jax import lax
from jax.experimental import pallas as pl
from jax.experimental.pallas import tpu as pltpu

F32 = jnp.float32
BF16 = jnp.bfloat16

D_MODEL = 2048
HEAD_DIM = 128
GRID_W = 64
NA_HEADS = 8
NA_KH = 8
NA_KW = 16
GQA_HEADS = 8
GQA_KV_HEADS = 2
GQA_GROUP = GQA_HEADS // GQA_KV_HEADS
ROPE_THETA = 10000.0
EPS = 1e-6
SCALE = HEAD_DIM ** -0.5

NA_W = NA_HEADS * HEAD_DIM
GQ_W = GQA_HEADS * HEAD_DIM
KV_W = GQA_KV_HEADS * HEAD_DIM
OFF_NA_Q = 0
OFF_NA_K = OFF_NA_Q + NA_W
OFF_NA_V = OFF_NA_K + NA_W
OFF_NA_Z = OFF_NA_V + NA_W
OFF_GQ_Q = OFF_NA_Z + NA_W
OFF_GQ_K = OFF_GQ_Q + GQ_W
OFF_GQ_V = OFF_GQ_K + KV_W
OFF_GQ_Z = OFF_GQ_V + KV_W
OFF_GA = OFF_GQ_Z + GQ_W
OFF_GB = OFF_GA + D_MODEL
IN_WIDTH = OFF_GB + D_MODEL


def _slab(off):
    return (off + 2 * D_MODEL) % IN_WIDTH

VMEM_LIMIT_BYTES = 56 * 1024 * 1024

PROJ_TM = 1024
PROJ_TN = 1536
PROJ_PAIR = 256
PROJ_WBLK = 512
NA_QROWS = 4
NA_TQ = NA_QROWS * GRID_W
NA_WCHUNKS = 3
NA_WROWS = NA_WCHUNKS * NA_QROWS
NA_TK = NA_WROWS * GRID_W
NA_NB = 8
NA_ONES_ROWS = 16
GQA_TQ = 512
GQA_TK = 512
GQA_ONES_ROWS = 16
EXP2_SCALE = SCALE * float(np.log2(np.e))
TAIL_TM = 256
NEG_BIG = -1e30


def _sigmoid(x):
    return 0.5 * jnp.tanh(0.5 * x) + 0.5


def _proj_kernel(x_ref, ng_ref, *rest):
    n_w = PROJ_TN // PROJ_WBLK
    w_refs, (hg_ref, gb_ref, o_ref, h_scr) = rest[:n_w], rest[n_w:]
    j = pl.program_id(1)

    @pl.when(j == 0)
    def _():
        x = x_ref[...]
        ms = jnp.mean(x * x, axis=-1, keepdims=True)
        h_scr[...] = (x * lax.rsqrt(ms + EPS) * ng_ref[...]).astype(BF16)

    def epilogue(col, a):
        kind, arg = _slab_head_kind(col)
        if kind == "gate":
            return _sigmoid(a + gb_ref[:, arg:arg + HEAD_DIM])
        if kind == "norm":
            ms = jnp.mean(a * a, axis=-1, keepdims=True)
            return a * lax.rsqrt(ms + EPS) * hg_ref[arg:arg + 1, :]
        if kind == "silu":
            return a * _sigmoid(a)
        return a

    for jj in range(IN_WIDTH // PROJ_TN):
        @pl.when(j == jj)
        def _(jj=jj):
            for pp in range(PROJ_TN // PROJ_PAIR):
                c0 = pp * PROJ_PAIR
                w_pair = w_refs[c0 // PROJ_WBLK][:, c0 % PROJ_WBLK:c0 % PROJ_WBLK + PROJ_PAIR]
                acc = jnp.dot(h_scr[...], w_pair, preferred_element_type=F32)
                for half in range(PROJ_PAIR // HEAD_DIM):
                    c = pp * PROJ_PAIR + half * HEAD_DIM
                    val = epilogue(jj * PROJ_TN + c, acc[:, half * HEAD_DIM:(half + 1) * HEAD_DIM])
                    o_ref[:, c:c + HEAD_DIM] = val.astype(o_ref.dtype)


def _slab_head_kind(col):
    groups = ((OFF_GA, 2 * D_MODEL, "gate", None), (OFF_NA_Q, NA_W, "norm", 0), (OFF_NA_K, NA_W, "norm", 1),
              (OFF_NA_V, NA_W, "copy", None), (OFF_NA_Z, NA_W, "silu", None), (OFF_GQ_Q, GQ_W, "norm", 2),
              (OFF_GQ_K, KV_W, "norm", 3), (OFF_GQ_V, KV_W, "copy", None), (OFF_GQ_Z, GQ_W, "silu", None))
    for off, width, kind, arg in groups:
        if _slab(off) <= col < _slab(off) + width:
            return kind, (col - _slab(off)) if kind == "gate" else arg
    raise ValueError(col)


def _proj(x2d, norm_g, w_in_bf, head_g, gate_bias):
    t = x2d.shape[0]
    tm = PROJ_TM
    assert t % tm == 0 and IN_WIDTH % PROJ_TN == 0 and PROJ_TN % PROJ_WBLK == 0
    assert PROJ_WBLK % PROJ_PAIR == 0 and OFF_GA % PROJ_WBLK == 0
    n_w = PROJ_TN // PROJ_WBLK
    n_wblk = IN_WIDTH // PROJ_WBLK

    def w_spec(u):
        return pl.BlockSpec((D_MODEL, PROJ_WBLK),
                            lambda i, j: (0, (n_w * j + u + OFF_GA // PROJ_WBLK) % n_wblk))

    return pl.pallas_call(
        _proj_kernel,
        name="proj",
        grid=(t // tm, IN_WIDTH // PROJ_TN),
        in_specs=[
            pl.BlockSpec((tm, D_MODEL), lambda i, j: (i, 0)),
            pl.BlockSpec((1, D_MODEL), lambda i, j: (0, 0)),
            *[w_spec(u) for u in range(n_w)],
            pl.BlockSpec((4, HEAD_DIM), lambda i, j: (0, 0)),
            pl.BlockSpec((1, 2 * D_MODEL), lambda i, j: (0, 0)),
        ],
        out_specs=pl.BlockSpec((tm, PROJ_TN), lambda i, j: (i, j)),
        out_shape=jax.ShapeDtypeStruct((t, IN_WIDTH), BF16),
        scratch_shapes=[pltpu.VMEM((tm, D_MODEL), BF16)],
        compiler_params=pltpu.CompilerParams(
            dimension_semantics=("parallel", "arbitrary"),
            vmem_limit_bytes=VMEM_LIMIT_BYTES),
    )(x2d, norm_g, *([w_in_bf] * n_w), head_g, gate_bias)


def _na_kernel(q_ref, k_ref, v_ref, z_ref, bias_ref, o_ref, vt_scr, s_scr, *, nblk):
    i = pl.program_id(2)

    @pl.when(i == 0)
    def _():
        def fill(c, carry):
            rows = pl.ds(pl.multiple_of(c * NA_TQ, NA_TQ), NA_TQ)
            vt_scr[c, 0:HEAD_DIM, :] = v_ref[rows, :].astype(F32).T.astype(BF16)
            vt_scr[c, HEAD_DIM:, :] = jnp.ones((NA_ONES_ROWS, NA_TQ), BF16)
            return carry
        lax.fori_loop(0, nblk, fill, 0)

    def first_chunk(nb):
        return jnp.clip(nb - 1, 0, nblk - NA_WCHUNKS)

    for jb in range(NA_NB):
        start = pl.multiple_of(first_chunk(i * NA_NB + jb) * NA_TQ, NA_TQ)
        k = k_ref[pl.ds(start, NA_TK), :]
        q = q_ref[jb * NA_TQ:(jb + 1) * NA_TQ, :]
        s_scr[jb] = lax.dot_general(k, q, (((1,), (1,)), ((), ())), preferred_element_type=F32)

    for jb in range(NA_NB):
        nb = i * NA_NB + jb
        chunk0 = first_chunk(nb)
        kind = jnp.where(nb == 0, 0, jnp.where(nb == nblk - 1, 2, 1))
        logits = EXP2_SCALE * s_scr[jb] + bias_ref[kind, 0]
        m = jnp.max(logits, axis=0, keepdims=True)
        p = jnp.exp2(logits - m).astype(BF16)
        pv = jnp.dot(vt_scr[chunk0], p[0:NA_TQ], preferred_element_type=F32)
        for w in range(1, NA_WCHUNKS):
            pv += jnp.dot(vt_scr[chunk0 + w], p[w * NA_TQ:(w + 1) * NA_TQ],
                          preferred_element_type=F32)
        o = (pv[0:HEAD_DIM] / pv[HEAD_DIM:HEAD_DIM + 1]).T
        rows = slice(jb * NA_TQ, (jb + 1) * NA_TQ)
        o_ref[rows, :] = (o * z_ref[rows, :].astype(F32)).astype(o_ref.dtype)


def _na_bias_table(rpb):
    c = np.arange(GRID_W)[:, None]
    kc = np.arange(GRID_W)[None, :]
    cs = np.clip(c - NA_KW // 2, 0, GRID_W - NA_KW)
    col_ok = (kc >= cs) & (kc < cs + NA_KW)
    dc = kc - c + NA_KW - 1
    onehot = ((dc.T[None] == np.arange(2 * NA_KW - 1)[:, None, None]) & col_ok.T[None]).astype(np.float32)
    t1 = jnp.einsum('hrd,dkc->hrkc', rpb.astype(F32), onehot, precision=lax.Precision.HIGHEST)
    t1 = jnp.where(col_ok.T[None, None], t1 * float(np.log2(np.e)), NEG_BIG)
    t1 = jnp.pad(t1, ((0, 0), (NA_BIAS_PAD, NA_BIAS_PAD), (0, 0), (0, 0)))
    t2 = jnp.concatenate([t1[:, 1:], t1[:, :-1]], axis=-1)
    return pl.pallas_call(
        _na_bias_kernel,
        name="na_bias",
        grid=(NA_HEADS, 3),
        in_specs=[pl.BlockSpec((1,) + t2.shape[1:], lambda h, k: (h, 0, 0, 0))],
        out_specs=pl.BlockSpec((1, 1, NA_TK, NA_TQ), lambda h, k: (k, h, 0, 0)),
        out_shape=jax.ShapeDtypeStruct((3, NA_HEADS, NA_TK, NA_TQ), F32),
        compiler_params=pltpu.CompilerParams(dimension_semantics=("parallel", "arbitrary")),
    )(t2)


NA_BIAS_PAD = NA_QROWS
NA_BLOCK_OFF = (0, NA_KH // 2, NA_WROWS - NA_QROWS)


def _na_row_ok(kind, a, jw):
    first_key = (0, a, NA_WROWS - NA_KH)[kind]
    return first_key <= jw < first_key + NA_KH


def _na_bias_kernel(t2_ref, o_ref):
    kind = pl.program_id(1)
    lane = lax.broadcasted_iota(jnp.int32, (GRID_W, 2 * GRID_W), 1)
    for k, off in enumerate(NA_BLOCK_OFF):
        @pl.when(kind == k)
        def _(k=k, off=off):
            for jw in range(NA_WROWS):
                for a in range(0, NA_QROWS, 2):
                    tile = t2_ref[0, jw - off - a + NA_KH - 1 + NA_BIAS_PAD - 1]
                    ok_a, ok_b = _na_row_ok(k, a, jw), _na_row_ok(k, a + 1, jw)
                    if ok_a and ok_b:
                        val = tile
                    elif ok_a:
                        val = jnp.where(lane < GRID_W, tile, NEG_BIG)
                    elif ok_b:
                        val = jnp.where(lane >= GRID_W, tile, NEG_BIG)
                    else:
                        val = jnp.full(tile.shape, NEG_BIG, F32)
                    o_ref[0, 0, jw * GRID_W:(jw + 1) * GRID_W, a * GRID_W:(a + 2) * GRID_W] = val


def _na(proj, batch, seq_len, bias_tbl):
    t = proj.shape[0]
    rows = seq_len // GRID_W
    nblk = rows // NA_QROWS
    assert rows % (NA_QROWS * NA_NB) == 0 and nblk >= NA_WCHUNKS
    nstep = nblk // NA_NB
    tq = NA_NB * NA_TQ
    cb = lambda off: _slab(off) // HEAD_DIM
    return pl.pallas_call(
        functools.partial(_na_kernel, nblk=nblk),
        name="na",
        grid=(NA_HEADS, batch, nstep),
        in_specs=[
            pl.BlockSpec((tq, HEAD_DIM), lambda h, b, i: (b * nstep + i, cb(OFF_NA_Q) + h)),
            pl.BlockSpec((seq_len, HEAD_DIM), lambda h, b, i: (b, cb(OFF_NA_K) + h)),
            pl.BlockSpec((seq_len, HEAD_DIM), lambda h, b, i: (b, cb(OFF_NA_V) + h)),
            pl.BlockSpec((tq, HEAD_DIM), lambda h, b, i: (b * nstep + i, cb(OFF_NA_Z) + h)),
            pl.BlockSpec((3, 1, NA_TK, NA_TQ), lambda h, b, i: (0, h, 0, 0)),
        ],
        out_specs=pl.BlockSpec((tq, HEAD_DIM), lambda h, b, i: (b * nstep + i, h)),
        out_shape=jax.ShapeDtypeStruct((t, NA_W), BF16),
        scratch_shapes=[
            pltpu.VMEM((nblk, HEAD_DIM + NA_ONES_ROWS, NA_TQ), BF16),
            pltpu.VMEM((NA_NB, NA_TK, NA_TQ), F32),
        ],
        compiler_params=pltpu.CompilerParams(
            dimension_semantics=("parallel", "parallel", "arbitrary"),
            vmem_limit_bytes=VMEM_LIMIT_BYTES),
    )(proj, proj, proj, proj, bias_tbl)


def _gqa_kernel(q_ref, k_ref, v_ref, z_ref, cos_ref, sin_ref, cost_ref, sint_ref, o_ref,
                qt_scr, k_scr, vt_scr, m_scr, acc_scr, sa_scr, sb_scr, *, nk):
    i = pl.program_id(2)
    quarter = HEAD_DIM // 4

    @pl.when(i == 0)
    def _():
        def fill(c, carry):
            start = pl.multiple_of(c * GQA_TK, GQA_TK)
            rows = pl.ds(start, GQA_TK)
            v = v_ref[rows, :].astype(F32)
            vt_scr[c, 0:HEAD_DIM, :] = v.T.astype(BF16)
            vt_scr[c, HEAD_DIM:, :] = jnp.ones((GQA_ONES_ROWS, GQA_TK), BF16)
            kk = k_ref[rows, :].astype(F32)
            lane = lax.broadcasted_iota(jnp.int32, kk.shape, 1)
            partner = jnp.where((lane % (2 * quarter)) < quarter,
                                pltpu.roll(kk, HEAD_DIM - quarter, 1),
                                pltpu.roll(kk, quarter, 1))
            k_scr[rows, :] = (kk * cos_ref[rows, :] + partner * sin_ref[rows, :]).astype(BF16)
            return carry
        lax.fori_loop(0, nk, fill, 0)

    for g in range(GQA_GROUP):
        qt = q_ref[:, g * HEAD_DIM:(g + 1) * HEAD_DIM].astype(F32).T
        partner = jnp.concatenate([qt[quarter:2 * quarter], qt[0:quarter],
                                   qt[3 * quarter:], qt[2 * quarter:3 * quarter]], axis=0)
        qt_scr[g] = (qt * cost_ref[...] + partner * sint_ref[...]).astype(BF16)
    m_scr[...] = jnp.full(m_scr.shape, -jnp.inf, F32)
    acc_scr[...] = jnp.zeros(acc_scr.shape, F32)

    def scores(c, s_buf):
        start = pl.multiple_of(c * GQA_TK, GQA_TK)
        k = k_scr[pl.ds(start, GQA_TK), :]
        for g in range(GQA_GROUP):
            s_buf[g] = jnp.dot(k, qt_scr[g], preferred_element_type=F32)

    def accumulate(c, s_buf):
        vt = vt_scr[c]
        for g in range(GQA_GROUP):
            s = s_buf[g]
            m_old = m_scr[g]
            m_new = jnp.maximum(m_old, jnp.max(s, axis=0, keepdims=True))
            alpha = jnp.exp2(EXP2_SCALE * (m_old - m_new))
            p = jnp.exp2(EXP2_SCALE * (s - m_new)).astype(BF16)
            acc_scr[g] = alpha * acc_scr[g] + jnp.dot(vt, p, preferred_element_type=F32)
            m_scr[g] = m_new

    scores(0, sa_scr)

    def pair(j, carry):
        scores(2 * j + 1, sb_scr)
        accumulate(2 * j, sa_scr)
        scores(2 * j + 2, sa_scr)
        accumulate(2 * j + 1, sb_scr)
        return carry

    lax.fori_loop(0, nk // 2 - 1, pair, 0)
    scores(nk - 1, sb_scr)
    accumulate(nk - 2, sa_scr)
    accumulate(nk - 1, sb_scr)
    for g in range(GQA_GROUP):
        cols = slice(g * HEAD_DIM, (g + 1) * HEAD_DIM)
        acc = acc_scr[g]
        o = (acc[0:HEAD_DIM] / acc[HEAD_DIM:HEAD_DIM + 1]).T
        o_ref[:, cols] = (o * z_ref[:, cols].astype(F32)).astype(o_ref.dtype)


def _gqa(proj, batch, seq_len, cos_t, sin_t, cos_tt, sin_tt):
    t = proj.shape[0]
    assert seq_len % GQA_TQ == 0 and seq_len % (2 * GQA_TK) == 0
    nq = seq_len // GQA_TQ
    gw = GQA_GROUP * HEAD_DIM
    return pl.pallas_call(
        functools.partial(_gqa_kernel, nk=seq_len // GQA_TK),
        name="gqa",
        grid=(batch, GQA_KV_HEADS, nq),
        in_specs=[
            pl.BlockSpec((GQA_TQ, gw), lambda b, g, i: (b * nq + i, _slab(OFF_GQ_Q) // gw + g)),
            pl.BlockSpec((seq_len, HEAD_DIM), lambda b, g, i: (b, _slab(OFF_GQ_K) // HEAD_DIM + g)),
            pl.BlockSpec((seq_len, HEAD_DIM), lambda b, g, i: (b, _slab(OFF_GQ_V) // HEAD_DIM + g)),
            pl.BlockSpec((GQA_TQ, gw), lambda b, g, i: (b * nq + i, _slab(OFF_GQ_Z) // gw + g)),
            pl.BlockSpec((seq_len, HEAD_DIM), lambda b, g, i: (0, 0), pipeline_mode=pl.Buffered(1)),
            pl.BlockSpec((seq_len, HEAD_DIM), lambda b, g, i: (0, 0), pipeline_mode=pl.Buffered(1)),
            pl.BlockSpec((HEAD_DIM, GQA_TQ), lambda b, g, i: (0, i)),
            pl.BlockSpec((HEAD_DIM, GQA_TQ), lambda b, g, i: (0, i)),
        ],
        out_specs=pl.BlockSpec((GQA_TQ, gw), lambda b, g, i: (b * nq + i, g)),
        out_shape=jax.ShapeDtypeStruct((t, GQ_W), BF16),
        scratch_shapes=[
            pltpu.VMEM((GQA_GROUP, HEAD_DIM, GQA_TQ), BF16),
            pltpu.VMEM((seq_len, HEAD_DIM), BF16),
            pltpu.VMEM((seq_len // GQA_TK, HEAD_DIM + GQA_ONES_ROWS, GQA_TK), BF16),
            pltpu.VMEM((GQA_GROUP, 1, GQA_TQ), F32),
            pltpu.VMEM((GQA_GROUP, HEAD_DIM + GQA_ONES_ROWS, GQA_TQ), F32),
            pltpu.VMEM((GQA_GROUP, GQA_TK, GQA_TQ), F32),
            pltpu.VMEM((GQA_GROUP, GQA_TK, GQA_TQ), F32),
        ],
        compiler_params=pltpu.CompilerParams(
            dimension_semantics=("parallel", "parallel", "arbitrary"),
            vmem_limit_bytes=VMEM_LIMIT_BYTES),
    )(proj, proj, proj, proj, cos_t, sin_t, cos_tt, sin_tt)


def _tail_kernel(x_ref, oa_ref, ob_ref, ga_ref, gb_ref, wa_ref, wb_ref, wo_ref, y_ref):
    pa = jnp.dot(oa_ref[...], wa_ref[...], preferred_element_type=F32)
    pb = jnp.dot(ob_ref[...], wb_ref[...], preferred_element_type=F32)
    merged = ga_ref[...].astype(F32) * pa + gb_ref[...].astype(F32) * pb
    y_ref[...] = x_ref[...] + jnp.dot(merged.astype(BF16), wo_ref[...], preferred_element_type=F32)


def _tail(x2d, oa, ob, proj, wa_bf, wb_bf, wo_bf):
    t = x2d.shape[0]
    assert t % TAIL_TM == 0
    resident = dict(pipeline_mode=pl.Buffered(1))
    return pl.pallas_call(
        _tail_kernel,
        name="tail",
        grid=(t // TAIL_TM,),
        in_specs=[
            pl.BlockSpec((TAIL_TM, D_MODEL), lambda i: (i, 0)),
            pl.BlockSpec((TAIL_TM, NA_W), lambda i: (i, 0)),
            pl.BlockSpec((TAIL_TM, GQ_W), lambda i: (i, 0)),
            pl.BlockSpec((TAIL_TM, D_MODEL), lambda i: (i, _slab(OFF_GA) // D_MODEL)),
            pl.BlockSpec((TAIL_TM, D_MODEL), lambda i: (i, _slab(OFF_GB) // D_MODEL)),
            pl.BlockSpec((NA_W, D_MODEL), lambda i: (0, 0), **resident),
            pl.BlockSpec((GQ_W, D_MODEL), lambda i: (0, 0), **resident),
            pl.BlockSpec((D_MODEL, D_MODEL), lambda i: (0, 0), **resident),
        ],
        out_specs=pl.BlockSpec((TAIL_TM, D_MODEL), lambda i: (i, 0)),
        out_shape=jax.ShapeDtypeStruct((t, D_MODEL), F32),
        compiler_params=pltpu.CompilerParams(
            dimension_semantics=("parallel",),
            vmem_limit_bytes=VMEM_LIMIT_BYTES),
    )(x2d, oa, ob, proj, proj, wa_bf, wb_bf, wo_bf)


def _rope_tables(seq_len):
    pos = np.arange(seq_len)
    axis = HEAD_DIM // 2
    freqs = ROPE_THETA ** (-np.arange(0, axis, 2, dtype=np.float64) / axis)
    ang_r = (pos // GRID_W)[:, None] * freqs[None, :]
    ang_c = (pos % GRID_W)[:, None] * freqs[None, :]
    cos_t = np.concatenate([np.cos(ang_r), np.cos(ang_r), np.cos(ang_c), np.cos(ang_c)], axis=-1)
    sin_t = np.concatenate([-np.sin(ang_r), np.sin(ang_r), -np.sin(ang_c), np.sin(ang_c)], axis=-1)
    tables = (cos_t, sin_t, cos_t.T, sin_t.T)
    return tuple(jnp.asarray(np.ascontiguousarray(tbl), dtype=F32) for tbl in tables)


def _layer(x, norm_g, w_in_bf, head_g, bias_tbl, wa_bf, wb_bf, gate_bias, wo_bf):
    batch, seq_len, _ = x.shape
    x2d = x.reshape(batch * seq_len, D_MODEL)
    proj = _proj(x2d, norm_g, w_in_bf, head_g, gate_bias)
    oa = _na(proj, batch, seq_len, bias_tbl)
    ob = _gqa(proj, batch, seq_len, *_rope_tables(seq_len))
    y = _tail(x2d, oa, ob, proj, wa_bf, wb_bf, wo_bf)
    return y.reshape(batch, seq_len, D_MODEL)


def kernel(x_prompt, x_sample, norm_g, w_in, na_q_g, na_k_g, na_rpb, gq_q_g, gq_k_g,
           w_branch_a, w_branch_b, gate_bias, w_out):
    depth = norm_g.shape[0]
    y_prompt, y_sample = x_prompt, x_sample
    for l in range(depth):
        head_g = jnp.stack([na_q_g[l], na_k_g[l], gq_q_g[l], gq_k_g[l]]).astype(F32)
        params = (norm_g[l][None, :].astype(F32), w_in[l].astype(BF16), head_g,
                  _na_bias_table(na_rpb[l]), w_branch_a[l].astype(BF16), w_branch_b[l].astype(BF16),
                  gate_bias[l][None, :].astype(F32), w_out[l].astype(BF16))
        y_prompt = _layer(y_prompt, *params)
        y_sample = _layer(y_sample, *params)
    return (y_prompt, y_sample)
```

```python
import functools

import numpy as np
import jax
import jax.numpy as jnp
from jax import lax
from jax.experimental import pallas as pl
from jax.experimental.pallas import tpu as pltpu

F32 = jnp.float32
BF16 = jnp.bfloat16

D_MODEL = 2048
HEAD_DIM = 128
GRID_W = 64
NA_HEADS = 8
NA_KH = 8
NA_KW = 16
GQA_HEADS = 8
GQA_KV_HEADS = 2
GQA_GROUP = GQA_HEADS // GQA_KV_HEADS
ROPE_THETA = 10000.0
EPS = 1e-6
SCALE = HEAD_DIM ** -0.5

NA_W = NA_HEADS * HEAD_DIM
GQ_W = GQA_HEADS * HEAD_DIM
KV_W = GQA_KV_HEADS * HEAD_DIM
OFF_NA_Q = 0
OFF_NA_K = OFF_NA_Q + NA_W
OFF_NA_V = OFF_NA_K + NA_W
OFF_NA_Z = OFF_NA_V + NA_W
OFF_GQ_Q = OFF_NA_Z + NA_W
OFF_GQ_K = OFF_GQ_Q + GQ_W
OFF_GQ_V = OFF_GQ_K + KV_W
OFF_GQ_Z = OFF_GQ_V + KV_W
OFF_GA = OFF_GQ_Z + GQ_W
OFF_GB = OFF_GA + D_MODEL
IN_WIDTH = OFF_GB + D_MODEL


def _slab(off):
    return (off + 2 * D_MODEL) % IN_WIDTH

VMEM_LIMIT_BYTES = 56 * 1024 * 1024

PROJ_TM = 1024
PROJ_TN = 1536
PROJ_PAIR = 256
PROJ_WBLK = 512
NA_QROWS = 4
NA_TQ = NA_QROWS * GRID_W
NA_WCHUNKS = 3
NA_WROWS = NA_WCHUNKS * NA_QROWS
NA_TK = NA_WROWS * GRID_W
NA_NB = 8
NA_ONES_ROWS = 16
GQA_TQ = 512
GQA_TK = 512
GQA_ONES_ROWS = 16
EXP2_SCALE = SCALE * float(np.log2(np.e))
TAIL_TM = 256
NEG_BIG = -1e30


def _sigmoid(x):
    return 0.5 * jnp.tanh(0.5 * x) + 0.5


def _proj_kernel(x_ref, ng_ref, *rest):
    n_w = PROJ_TN // PROJ_WBLK
    w_refs, (hg_ref, gb_ref, o_ref, h_scr) = rest[:n_w], rest[n_w:]
    j = pl.program_id(1)

    @pl.when(j == 0)
    def _():
        x = x_ref[...]
        ms = jnp.mean(x * x, axis=-1, keepdims=True)
        h_scr[...] = (x * lax.rsqrt(ms + EPS) * ng_ref[...]).astype(BF16)

    def epilogue(col, a):
        kind, arg = _slab_head_kind(col)
        if kind == "gate":
            return _sigmoid(a + gb_ref[:, arg:arg + HEAD_DIM])
        if kind == "norm":
            ms = jnp.mean(a * a, axis=-1, keepdims=True)
            return a * lax.rsqrt(ms + EPS) * hg_ref[arg:arg + 1, :]
        if kind == "silu":
            return a * _sigmoid(a)
        return a

    for jj in range(IN_WIDTH // PROJ_TN):
        @pl.when(j == jj)
        def _(jj=jj):
            for pp in range(PROJ_TN // PROJ_PAIR):
                c0 = pp * PROJ_PAIR
                w_pair = w_refs[c0 // PROJ_WBLK][:, c0 % PROJ_WBLK:c0 % PROJ_WBLK + PROJ_PAIR]
                acc = jnp.dot(h_scr[...], w_pair, preferred_element_type=F32)
                for half in range(PROJ_PAIR // HEAD_DIM):
                    c = pp * PROJ_PAIR + half * HEAD_DIM
                    val = epilogue(jj * PROJ_TN + c, acc[:, half * HEAD_DIM:(half + 1) * HEAD_DIM])
                    o_ref[c // HEAD_DIM] = val.astype(o_ref.dtype)


def _slab_head_kind(col):
    groups = ((OFF_GA, 2 * D_MODEL, "gate", None), (OFF_NA_Q, NA_W, "norm", 0), (OFF_NA_K, NA_W, "norm", 1),
              (OFF_NA_V, NA_W, "copy", None), (OFF_NA_Z, NA_W, "silu", None), (OFF_GQ_Q, GQ_W, "norm", 2),
              (OFF_GQ_K, KV_W, "norm", 3), (OFF_GQ_V, KV_W, "copy", None), (OFF_GQ_Z, GQ_W, "silu", None))
    for off, width, kind, arg in groups:
        if _slab(off) <= col < _slab(off) + width:
            return kind, (col - _slab(off)) if kind == "gate" else arg
    raise ValueError(col)


def _proj(x2d, norm_g, w_in_bf, head_g, gate_bias):
    t = x2d.shape[0]
    tm = PROJ_TM
    assert t % tm == 0 and IN_WIDTH % PROJ_TN == 0 and PROJ_TN % PROJ_WBLK == 0
    assert PROJ_WBLK % PROJ_PAIR == 0 and OFF_GA % PROJ_WBLK == 0
    n_w = PROJ_TN // PROJ_WBLK
    n_wblk = IN_WIDTH // PROJ_WBLK

    def w_spec(u):
        return pl.BlockSpec((D_MODEL, PROJ_WBLK),
                            lambda i, j: (0, (n_w * j + u + OFF_GA // PROJ_WBLK) % n_wblk))

    return pl.pallas_call(
        _proj_kernel,
        name="proj",
        grid=(t // tm, IN_WIDTH // PROJ_TN),
        in_specs=[
            pl.BlockSpec((tm, D_MODEL), lambda i, j: (i, 0)),
            pl.BlockSpec((1, D_MODEL), lambda i, j: (0, 0)),
            *[w_spec(u) for u in range(n_w)],
            pl.BlockSpec((4, HEAD_DIM), lambda i, j: (0, 0)),
            pl.BlockSpec((1, 2 * D_MODEL), lambda i, j: (0, 0)),
        ],
        out_specs=pl.BlockSpec((PROJ_TN // HEAD_DIM, tm, HEAD_DIM), lambda i, j: (j, i, 0)),
        out_shape=jax.ShapeDtypeStruct((IN_WIDTH // HEAD_DIM, t, HEAD_DIM), BF16),
        scratch_shapes=[pltpu.VMEM((tm, D_MODEL), BF16)],
        compiler_params=pltpu.CompilerParams(
            dimension_semantics=("parallel", "arbitrary"),
            vmem_limit_bytes=VMEM_LIMIT_BYTES),
    )(x2d, norm_g, *([w_in_bf] * n_w), head_g, gate_bias)


def _na_kernel(q_ref, k_ref, v_ref, z_ref, bias_ref, o_ref, vt_scr, s_scr, *, nblk):
    i = pl.program_id(2)

    @pl.when(i == 0)
    def _():
        def fill(c, carry):
            rows = pl.ds(pl.multiple_of(c * NA_TQ, NA_TQ), NA_TQ)
            vt_scr[c, 0:HEAD_DIM, :] = v_ref[rows, :].astype(F32).T.astype(BF16)
            vt_scr[c, HEAD_DIM:, :] = jnp.ones((NA_ONES_ROWS, NA_TQ), BF16)
            return carry
        lax.fori_loop(0, nblk, fill, 0)

    def first_chunk(nb):
        return jnp.clip(nb - 1, 0, nblk - NA_WCHUNKS)

    for jb in range(NA_NB):
        start = pl.multiple_of(first_chunk(i * NA_NB + jb) * NA_TQ, NA_TQ)
        k = k_ref[pl.ds(start, NA_TK), :]
        q = q_ref[jb * NA_TQ:(jb + 1) * NA_TQ, :]
        s_scr[jb] = lax.dot_general(k, q, (((1,), (1,)), ((), ())), preferred_element_type=F32)

    for jb in range(NA_NB):
        nb = i * NA_NB + jb
        chunk0 = first_chunk(nb)
        kind = jnp.where(nb == 0, 0, jnp.where(nb == nblk - 1, 2, 1))
        logits = EXP2_SCALE * s_scr[jb] + bias_ref[kind, 0]
        m = jnp.max(logits, axis=0, keepdims=True)
        p = jnp.exp2(logits - m).astype(BF16)
        pv = jnp.dot(vt_scr[chunk0], p[0:NA_TQ], preferred_element_type=F32)
        for w in range(1, NA_WCHUNKS):
            pv += jnp.dot(vt_scr[chunk0 + w], p[w * NA_TQ:(w + 1) * NA_TQ],
                          preferred_element_type=F32)
        o = (pv[0:HEAD_DIM] / pv[HEAD_DIM:HEAD_DIM + 1]).T
        rows = slice(jb * NA_TQ, (jb + 1) * NA_TQ)
        o_ref[rows, :] = (o * z_ref[rows, :].astype(F32)).astype(o_ref.dtype)


def _na_bias_table(rpb):
    c = np.arange(GRID_W)[:, None]
    kc = np.arange(GRID_W)[None, :]
    cs = np.clip(c - NA_KW // 2, 0, GRID_W - NA_KW)
    col_ok = (kc >= cs) & (kc < cs + NA_KW)
    dc = kc - c + NA_KW - 1
    onehot = ((dc.T[None] == np.arange(2 * NA_KW - 1)[:, None, None]) & col_ok.T[None]).astype(np.float32)
    t1 = jnp.einsum('hrd,dkc->hrkc', rpb.astype(F32), onehot, precision=lax.Precision.HIGHEST)
    t1 = jnp.where(col_ok.T[None, None], t1 * float(np.log2(np.e)), NEG_BIG)
    t1 = jnp.pad(t1, ((0, 0), (NA_BIAS_PAD, NA_BIAS_PAD), (0, 0), (0, 0)))
    t2 = jnp.concatenate([t1[:, 1:], t1[:, :-1]], axis=-1)
    return pl.pallas_call(
        _na_bias_kernel,
        name="na_bias",
        grid=(NA_HEADS, 3),
        in_specs=[pl.BlockSpec((1,) + t2.shape[1:], lambda h, k: (h, 0, 0, 0))],
        out_specs=pl.BlockSpec((1, 1, NA_TK, NA_TQ), lambda h, k: (k, h, 0, 0)),
        out_shape=jax.ShapeDtypeStruct((3, NA_HEADS, NA_TK, NA_TQ), F32),
        compiler_params=pltpu.CompilerParams(dimension_semantics=("parallel", "arbitrary")),
    )(t2)


NA_BIAS_PAD = NA_QROWS
NA_BLOCK_OFF = (0, NA_KH // 2, NA_WROWS - NA_QROWS)


def _na_row_ok(kind, a, jw):
    first_key = (0, a, NA_WROWS - NA_KH)[kind]
    return first_key <= jw < first_key + NA_KH


def _na_bias_kernel(t2_ref, o_ref):
    kind = pl.program_id(1)
    lane = lax.broadcasted_iota(jnp.int32, (GRID_W, 2 * GRID_W), 1)
    for k, off in enumerate(NA_BLOCK_OFF):
        @pl.when(kind == k)
        def _(k=k, off=off):
            for jw in range(NA_WROWS):
                for a in range(0, NA_QROWS, 2):
                    tile = t2_ref[0, jw - off - a + NA_KH - 1 + NA_BIAS_PAD - 1]
                    ok_a, ok_b = _na_row_ok(k, a, jw), _na_row_ok(k, a + 1, jw)
                    if ok_a and ok_b:
                        val = tile
                    elif ok_a:
                        val = jnp.where(lane < GRID_W, tile, NEG_BIG)
                    elif ok_b:
                        val = jnp.where(lane >= GRID_W, tile, NEG_BIG)
                    else:
                        val = jnp.full(tile.shape, NEG_BIG, F32)
                    o_ref[0, 0, jw * GRID_W:(jw + 1) * GRID_W, a * GRID_W:(a + 2) * GRID_W] = val


def _na(proj, batch, seq_len, bias_tbl):
    t = proj.shape[1]
    rows = seq_len // GRID_W
    nblk = rows // NA_QROWS
    assert rows % (NA_QROWS * NA_NB) == 0 and nblk >= NA_WCHUNKS
    nstep = nblk // NA_NB
    tq = NA_NB * NA_TQ
    cb = lambda off: _slab(off) // HEAD_DIM
    return pl.pallas_call(
        functools.partial(_na_kernel, nblk=nblk),
        name="na",
        grid=(NA_HEADS, batch, nstep),
        in_specs=[
            pl.BlockSpec((None, tq, HEAD_DIM), lambda h, b, i: (cb(OFF_NA_Q) + h, b * nstep + i, 0)),
            pl.BlockSpec((None, seq_len, HEAD_DIM), lambda h, b, i: (cb(OFF_NA_K) + h, b, 0)),
            pl.BlockSpec((None, seq_len, HEAD_DIM), lambda h, b, i: (cb(OFF_NA_V) + h, b, 0)),
            pl.BlockSpec((None, tq, HEAD_DIM), lambda h, b, i: (cb(OFF_NA_Z) + h, b * nstep + i, 0)),
            pl.BlockSpec((3, 1, NA_TK, NA_TQ), lambda h, b, i: (0, h, 0, 0)),
        ],
        out_specs=pl.BlockSpec((None, tq, HEAD_DIM), lambda h, b, i: (h, b * nstep + i, 0)),
        out_shape=jax.ShapeDtypeStruct((NA_HEADS, t, HEAD_DIM), BF16),
        scratch_shapes=[
            pltpu.VMEM((nblk, HEAD_DIM + NA_ONES_ROWS, NA_TQ), BF16),
            pltpu.VMEM((NA_NB, NA_TK, NA_TQ), F32),
        ],
        compiler_params=pltpu.CompilerParams(
            dimension_semantics=("parallel", "parallel", "arbitrary"),
            vmem_limit_bytes=VMEM_LIMIT_BYTES),
    )(proj, proj, proj, proj, bias_tbl)


def _gqa_kernel(q_ref, k_ref, v_ref, z_ref, cos_ref, sin_ref, cost_ref, sint_ref, o_ref,
                qt_scr, k_scr, vt_scr, m_scr, acc_scr, sa_scr, sb_scr, *, nk):
    i = pl.program_id(2)
    quarter = HEAD_DIM // 4

    @pl.when(i == 0)
    def _():
        def fill(c, carry):
            start = pl.multiple_of(c * GQA_TK, GQA_TK)
            rows = pl.ds(start, GQA_TK)
            v = v_ref[rows, :].astype(F32)
            vt_scr[c, 0:HEAD_DIM, :] = v.T.astype(BF16)
            vt_scr[c, HEAD_DIM:, :] = jnp.ones((GQA_ONES_ROWS, GQA_TK), BF16)
            kk = k_ref[rows, :].astype(F32)
            lane = lax.broadcasted_iota(jnp.int32, kk.shape, 1)
            partner = jnp.where((lane % (2 * quarter)) < quarter,
                                pltpu.roll(kk, HEAD_DIM - quarter, 1),
                                pltpu.roll(kk, quarter, 1))
            k_scr[rows, :] = (kk * cos_ref[rows, :] + partner * sin_ref[rows, :]).astype(BF16)
            return carry
        lax.fori_loop(0, nk, fill, 0)

    for g in range(GQA_GROUP):
        qt = q_ref[g].astype(F32).T
        partner = jnp.concatenate([qt[quarter:2 * quarter], qt[0:quarter],
                                   qt[3 * quarter:], qt[2 * quarter:3 * quarter]], axis=0)
        qt_scr[g] = (qt * cost_ref[...] + partner * sint_ref[...]).astype(BF16)
    m_scr[...] = jnp.full(m_scr.shape, -jnp.inf, F32)
    acc_scr[...] = jnp.zeros(acc_scr.shape, F32)

    def scores(c, s_buf):
        start = pl.multiple_of(c * GQA_TK, GQA_TK)
        k = k_scr[pl.ds(start, GQA_TK), :]
        for g in range(GQA_GROUP):
            s_buf[g] = jnp.dot(k, qt_scr[g], preferred_element_type=F32)

    def accumulate(c, s_buf):
        vt = vt_scr[c]
        for g in range(GQA_GROUP):
            s = s_buf[g]
            m_old = m_scr[g]
            m_new = jnp.maximum(m_old, jnp.max(s, axis=0, keepdims=True))
            alpha = jnp.exp2(EXP2_SCALE * (m_old - m_new))
            p = jnp.exp2(EXP2_SCALE * (s - m_new)).astype(BF16)
            acc_scr[g] = alpha * acc_scr[g] + jnp.dot(vt, p, preferred_element_type=F32)
            m_scr[g] = m_new

    scores(0, sa_scr)

    def pair(j, carry):
        scores(2 * j + 1, sb_scr)
        accumulate(2 * j, sa_scr)
        scores(2 * j + 2, sa_scr)
        accumulate(2 * j + 1, sb_scr)
        return carry

    lax.fori_loop(0, nk // 2 - 1, pair, 0)
    scores(nk - 1, sb_scr)
    accumulate(nk - 2, sa_scr)
    accumulate(nk - 1, sb_scr)
    for g in range(GQA_GROUP):
        cols = slice(g * HEAD_DIM, (g + 1) * HEAD_DIM)
        acc = acc_scr[g]
        o = (acc[0:HEAD_DIM] / acc[HEAD_DIM:HEAD_DIM + 1]).T
        o_ref[:, cols] = (o * z_ref[g].astype(F32)).astype(o_ref.dtype)


def _gqa(proj, batch, seq_len, cos_t, sin_t, cos_tt, sin_tt):
    t = proj.shape[1]
    assert seq_len % GQA_TQ == 0 and seq_len % (2 * GQA_TK) == 0
    nq = seq_len // GQA_TQ
    gw = GQA_GROUP * HEAD_DIM
    return pl.pallas_call(
        functools.partial(_gqa_kernel, nk=seq_len // GQA_TK),
        name="gqa",
        grid=(batch, GQA_KV_HEADS, nq),
        in_specs=[
            pl.BlockSpec((GQA_GROUP, GQA_TQ, HEAD_DIM), lambda b, g, i: (_slab(OFF_GQ_Q) // gw + g, b * nq + i, 0)),
            pl.BlockSpec((None, seq_len, HEAD_DIM), lambda b, g, i: (_slab(OFF_GQ_K) // HEAD_DIM + g, b, 0)),
            pl.BlockSpec((None, seq_len, HEAD_DIM), lambda b, g, i: (_slab(OFF_GQ_V) // HEAD_DIM + g, b, 0)),
            pl.BlockSpec((GQA_GROUP, GQA_TQ, HEAD_DIM), lambda b, g, i: (_slab(OFF_GQ_Z) // gw + g, b * nq + i, 0)),
            pl.BlockSpec((seq_len, HEAD_DIM), lambda b, g, i: (0, 0), pipeline_mode=pl.Buffered(1)),
            pl.BlockSpec((seq_len, HEAD_DIM), lambda b, g, i: (0, 0), pipeline_mode=pl.Buffered(1)),
            pl.BlockSpec((HEAD_DIM, GQA_TQ), lambda b, g, i: (0, i)),
            pl.BlockSpec((HEAD_DIM, GQA_TQ), lambda b, g, i: (0, i)),
        ],
        out_specs=pl.BlockSpec((GQA_TQ, gw), lambda b, g, i: (b * nq + i, g)),
        out_shape=jax.ShapeDtypeStruct((t, GQ_W), BF16),
        scratch_shapes=[
            pltpu.VMEM((GQA_GROUP, HEAD_DIM, GQA_TQ), BF16),
            pltpu.VMEM((seq_len, HEAD_DIM), BF16),
            pltpu.VMEM((seq_len // GQA_TK, HEAD_DIM + GQA_ONES_ROWS, GQA_TK), BF16),
            pltpu.VMEM((GQA_GROUP, 1, GQA_TQ), F32),
            pltpu.VMEM((GQA_GROUP, HEAD_DIM + GQA_ONES_ROWS, GQA_TQ), F32),
            pltpu.VMEM((GQA_GROUP, GQA_TK, GQA_TQ), F32),
            pltpu.VMEM((GQA_GROUP, GQA_TK, GQA_TQ), F32),
        ],
        compiler_params=pltpu.CompilerParams(
            dimension_semantics=("parallel", "parallel", "arbitrary"),
            vmem_limit_bytes=VMEM_LIMIT_BYTES),
    )(proj, proj, proj, proj, cos_t, sin_t, cos_tt, sin_tt)


def _tail_kernel(x_ref, oa_ref, ob_ref, ga_ref, gb_ref, wa_ref, wb_ref, wo_ref, y_ref):
    oa = jnp.concatenate([oa_ref[h] for h in range(NA_HEADS)], axis=1)
    pa = jnp.dot(oa, wa_ref[...], preferred_element_type=F32)
    pb = jnp.dot(ob_ref[...], wb_ref[...], preferred_element_type=F32)
    merged = jnp.concatenate(
        [(ga_ref[c].astype(F32) * pa[:, c * HEAD_DIM:(c + 1) * HEAD_DIM]
          + gb_ref[c].astype(F32) * pb[:, c * HEAD_DIM:(c + 1) * HEAD_DIM]).astype(BF16)
         for c in range(D_MODEL // HEAD_DIM)], axis=1)
    y_ref[...] = x_ref[...] + jnp.dot(merged, wo_ref[...], preferred_element_type=F32)


def _tail(x2d, oa, ob, proj, wa_bf, wb_bf, wo_bf):
    t = x2d.shape[0]
    assert t % TAIL_TM == 0
    resident = dict(pipeline_mode=pl.Buffered(1))
    return pl.pallas_call(
        _tail_kernel,
        name="tail",
        grid=(t // TAIL_TM,),
        in_specs=[
            pl.BlockSpec((TAIL_TM, D_MODEL), lambda i: (i, 0)),
            pl.BlockSpec((NA_HEADS, TAIL_TM, HEAD_DIM), lambda i: (0, i, 0)),
            pl.BlockSpec((TAIL_TM, GQ_W), lambda i: (i, 0)),
            pl.BlockSpec((D_MODEL // HEAD_DIM, TAIL_TM, HEAD_DIM), lambda i: (_slab(OFF_GA) // D_MODEL, i, 0)),
            pl.BlockSpec((D_MODEL // HEAD_DIM, TAIL_TM, HEAD_DIM), lambda i: (_slab(OFF_GB) // D_MODEL, i, 0)),
            pl.BlockSpec((NA_W, D_MODEL), lambda i: (0, 0), **resident),
            pl.BlockSpec((GQ_W, D_MODEL), lambda i: (0, 0), **resident),
            pl.BlockSpec((D_MODEL, D_MODEL), lambda i: (0, 0), **resident),
        ],
        out_specs=pl.BlockSpec((TAIL_TM, D_MODEL), lambda i: (i, 0)),
        out_shape=jax.ShapeDtypeStruct((t, D_MODEL), F32),
        compiler_params=pltpu.CompilerParams(
            dimension_semantics=("parallel",),
            vmem_limit_bytes=VMEM_LIMIT_BYTES),
    )(x2d, oa, ob, proj, proj, wa_bf, wb_bf, wo_bf)


def _rope_tables(seq_len):
    pos = np.arange(seq_len)
    axis = HEAD_DIM // 2
    freqs = ROPE_THETA ** (-np.arange(0, axis, 2, dtype=np.float64) / axis)
    ang_r = (pos // GRID_W)[:, None] * freqs[None, :]
    ang_c = (pos % GRID_W)[:, None] * freqs[None, :]
    cos_t = np.concatenate([np.cos(ang_r), np.cos(ang_r), np.cos(ang_c), np.cos(ang_c)], axis=-1)
    sin_t = np.concatenate([-np.sin(ang_r), np.sin(ang_r), -np.sin(ang_c), np.sin(ang_c)], axis=-1)
    tables = (cos_t, sin_t, cos_t.T, sin_t.T)
    return tuple(jnp.asarray(np.ascontiguousarray(tbl), dtype=F32) for tbl in tables)


def _layer(x, norm_g, w_in_bf, head_g, bias_tbl, wa_bf, wb_bf, gate_bias, wo_bf):
    batch, seq_len, _ = x.shape
    x2d = x.reshape(batch * seq_len, D_MODEL)
    proj = _proj(x2d, norm_g, w_in_bf, head_g, gate_bias)
    oa = _na(proj, batch, seq_len, bias_tbl)
    ob = _gqa(proj, batch, seq_len, *_rope_tables(seq_len))
    y = _tail(x2d, oa, ob, proj, wa_bf, wb_bf, wo_bf)
    return y.reshape(batch, seq_len, D_MODEL)


def kernel(x_prompt, x_sample, norm_g, w_in, na_q_g, na_k_g, na_rpb, gq_q_g, gq_k_g,
           w_branch_a, w_branch_b, gate_bias, w_out):
    depth = norm_g.shape[0]
    y_prompt, y_sample = x_prompt, x_sample
    for l in range(depth):
        head_g = jnp.stack([na_q_g[l], na_k_g[l], gq_q_g[l], gq_k_g[l]]).astype(F32)
        params = (norm_g[l][None, :].astype(F32), w_in[l].astype(BF16), head_g,
                  _na_bias_table(na_rpb[l]), w_branch_a[l].astype(BF16), w_branch_b[l].astype(BF16),
                  gate_bias[l][None, :].astype(F32), w_out[l].astype(BF16))
        y_prompt = _layer(y_prompt, *params)
        y_sample = _layer(y_sample, *params)
    return (y_prompt, y_sample)
```

```python
import functools

import numpy as np
import jax
import jax.numpy as jnp
from jax import lax
from jax.experimental import pallas as pl
from jax.experimental.pallas import tpu as pltpu

F32 = jnp.float32
BF16 = jnp.bfloat16

D_MODEL = 2048
HEAD_DIM = 128
GRID_W = 64
NA_HEADS = 8
NA_KH = 8
NA_KW = 16
GQA_HEADS = 8
GQA_KV_HEADS = 2
GQA_GROUP = GQA_HEADS // GQA_KV_HEADS
ROPE_THETA = 10000.0
EPS = 1e-6
SCALE = HEAD_DIM ** -0.5

NA_W = NA_HEADS * HEAD_DIM
GQ_W = GQA_HEADS * HEAD_DIM
KV_W = GQA_KV_HEADS * HEAD_DIM
OFF_NA_Q = 0
OFF_NA_K = OFF_NA_Q + NA_W
OFF_NA_V = OFF_NA_K + NA_W
OFF_NA_Z = OFF_NA_V + NA_W
OFF_GQ_Q = OFF_NA_Z + NA_W
OFF_GQ_K = OFF_GQ_Q + GQ_W
OFF_GQ_V = OFF_GQ_K + KV_W
OFF_GQ_Z = OFF_GQ_V + KV_W
OFF_GA = OFF_GQ_Z + GQ_W
OFF_GB = OFF_GA + D_MODEL
IN_WIDTH = OFF_GB + D_MODEL


def _slab(off):
    return (off + 2 * D_MODEL) % IN_WIDTH

VMEM_LIMIT_BYTES = 56 * 1024 * 1024

PROJ_TM = 1024
PROJ_TN = 1536
PROJ_PAIR = 256
PROJ_WBLK = 512
NA_QROWS = 4
NA_TQ = NA_QROWS * GRID_W
NA_WCHUNKS = 3
NA_WROWS = NA_WCHUNKS * NA_QROWS
NA_TK = NA_WROWS * GRID_W
NA_NB = 8
NA_ONES_ROWS = 16
GQA_TQ = 512
GQA_TK = 512
GQA_ONES_ROWS = 16
EXP2_SCALE = SCALE * float(np.log2(np.e))
TAIL_TM = 256
NEG_BIG = -1e30


def _sigmoid(x):
    return 0.5 * jnp.tanh(0.5 * x) + 0.5


def _proj_kernel(x_ref, ng_ref, *rest):
    n_w = PROJ_TN // PROJ_WBLK
    w_refs, (hg_ref, gb_ref, o_ref, h_scr) = rest[:n_w], rest[n_w:]
    j = pl.program_id(1)

    @pl.when(j == 0)
    def _():
        x = x_ref[...]
        ms = jnp.mean(x * x, axis=-1, keepdims=True)
        h_scr[...] = (x * lax.rsqrt(ms + EPS) * ng_ref[...]).astype(BF16)

    def epilogue(col, a):
        kind, arg = _slab_head_kind(col)
        if kind == "gate":
            return _sigmoid(a + gb_ref[:, arg:arg + HEAD_DIM])
        if kind == "norm":
            ms = jnp.mean(a * a, axis=-1, keepdims=True)
            return a * lax.rsqrt(ms + EPS) * hg_ref[arg:arg + 1, :]
        if kind == "silu":
            return a * _sigmoid(a)
        return a

    for jj in range(IN_WIDTH // PROJ_TN):
        @pl.when(j == jj)
        def _(jj=jj):
            for pp in range(PROJ_TN // PROJ_PAIR):
                c0 = pp * PROJ_PAIR
                w_pair = w_refs[c0 // PROJ_WBLK][:, c0 % PROJ_WBLK:c0 % PROJ_WBLK + PROJ_PAIR]
                acc = jnp.dot(h_scr[...], w_pair, preferred_element_type=F32)
                for half in range(PROJ_PAIR // HEAD_DIM):
                    c = pp * PROJ_PAIR + half * HEAD_DIM
                    val = epilogue(jj * PROJ_TN + c, acc[:, half * HEAD_DIM:(half + 1) * HEAD_DIM])
                    o_ref[c // HEAD_DIM] = val.astype(o_ref.dtype)


def _slab_head_kind(col):
    groups = ((OFF_GA, 2 * D_MODEL, "gate", None), (OFF_NA_Q, NA_W, "norm", 0), (OFF_NA_K, NA_W, "norm", 1),
              (OFF_NA_V, NA_W, "copy", None), (OFF_NA_Z, NA_W, "silu", None), (OFF_GQ_Q, GQ_W, "norm", 2),
              (OFF_GQ_K, KV_W, "norm", 3), (OFF_GQ_V, KV_W, "copy", None), (OFF_GQ_Z, GQ_W, "silu", None))
    for off, width, kind, arg in groups:
        if _slab(off) <= col < _slab(off) + width:
            return kind, (col - _slab(off)) if kind == "gate" else arg
    raise ValueError(col)


def _proj(x2d, norm_g, w_in_bf, head_g, gate_bias):
    t = x2d.shape[0]
    tm = PROJ_TM
    assert t % tm == 0 and IN_WIDTH % PROJ_TN == 0 and PROJ_TN % PROJ_WBLK == 0
    assert PROJ_WBLK % PROJ_PAIR == 0 and OFF_GA % PROJ_WBLK == 0
    n_w = PROJ_TN // PROJ_WBLK
    n_wblk = IN_WIDTH // PROJ_WBLK

    def w_spec(u):
        return pl.BlockSpec((D_MODEL, PROJ_WBLK),
                            lambda i, j: (0, (n_w * j + u + OFF_GA // PROJ_WBLK) % n_wblk))

    return pl.pallas_call(
        _proj_kernel,
        name="proj",
        grid=(t // tm, IN_WIDTH // PROJ_TN),
        in_specs=[
            pl.BlockSpec((tm, D_MODEL), lambda i, j: (i, 0)),
            pl.BlockSpec((1, D_MODEL), lambda i, j: (0, 0)),
            *[w_spec(u) for u in range(n_w)],
            pl.BlockSpec((4, HEAD_DIM), lambda i, j: (0, 0)),
            pl.BlockSpec((1, 2 * D_MODEL), lambda i, j: (0, 0)),
        ],
        out_specs=pl.BlockSpec((PROJ_TN // HEAD_DIM, tm, HEAD_DIM), lambda i, j: (j, i, 0)),
        out_shape=jax.ShapeDtypeStruct((IN_WIDTH // HEAD_DIM, t, HEAD_DIM), BF16),
        scratch_shapes=[pltpu.VMEM((tm, D_MODEL), BF16)],
        compiler_params=pltpu.CompilerParams(
            dimension_semantics=("parallel", "arbitrary"),
            vmem_limit_bytes=VMEM_LIMIT_BYTES),
    )(x2d, norm_g, *([w_in_bf] * n_w), head_g, gate_bias)


def _na_kernel(q_ref, k_ref, v_ref, z_ref, bias_ref, o_ref, vt_scr, s_scr, *, nblk):
    i = pl.program_id(2)

    def transpose_values(c):
        rows = pl.ds(pl.multiple_of(c * NA_TQ, NA_TQ), NA_TQ)
        vt_scr[c, 0:HEAD_DIM, :] = v_ref[rows, :].astype(F32).T.astype(BF16)
        vt_scr[c, HEAD_DIM:, :] = jnp.ones((NA_ONES_ROWS, NA_TQ), BF16)

    @pl.when(i == 0)
    def _():
        transpose_values(0)

    for jb in range(NA_NB):
        transpose_values(jnp.minimum(i * NA_NB + jb + 1, nblk - 1))

    def first_chunk(nb):
        return jnp.clip(nb - 1, 0, nblk - NA_WCHUNKS)

    def block_scores(jb):
        start = pl.multiple_of(first_chunk(i * NA_NB + jb) * NA_TQ, NA_TQ)
        k = k_ref[pl.ds(start, NA_TK), :]
        q = q_ref[jb * NA_TQ:(jb + 1) * NA_TQ, :]
        s_scr[jb] = lax.dot_general(k, q, (((1,), (1,)), ((), ())), preferred_element_type=F32)

    for jb in range(NA_NB):
        block_scores(jb)

    for jb in range(NA_NB):
        nb = i * NA_NB + jb
        chunk0 = first_chunk(nb)
        kind = jnp.where(nb == 0, 0, jnp.where(nb == nblk - 1, 2, 1))
        logits = EXP2_SCALE * s_scr[jb] + bias_ref[kind, 0]
        m = jnp.max(logits, axis=0, keepdims=True)
        p = jnp.exp2(logits - m).astype(BF16)
        pv = jnp.dot(vt_scr[chunk0], p[0:NA_TQ], preferred_element_type=F32)
        for w in range(1, NA_WCHUNKS):
            pv += jnp.dot(vt_scr[chunk0 + w], p[w * NA_TQ:(w + 1) * NA_TQ],
                          preferred_element_type=F32)
        o = (pv[0:HEAD_DIM] / pv[HEAD_DIM:HEAD_DIM + 1]).T
        rows = slice(jb * NA_TQ, (jb + 1) * NA_TQ)
        o_ref[rows, :] = (o * z_ref[rows, :].astype(F32)).astype(o_ref.dtype)


def _na_bias_table(rpb):
    c = np.arange(GRID_W)[:, None]
    kc = np.arange(GRID_W)[None, :]
    cs = np.clip(c - NA_KW // 2, 0, GRID_W - NA_KW)
    col_ok = (kc >= cs) & (kc < cs + NA_KW)
    dc = kc - c + NA_KW - 1
    onehot = ((dc.T[None] == np.arange(2 * NA_KW - 1)[:, None, None]) & col_ok.T[None]).astype(np.float32)
    t1 = jnp.einsum('hrd,dkc->hrkc', rpb.astype(F32), onehot, precision=lax.Precision.HIGHEST)
    t1 = jnp.where(col_ok.T[None, None], t1 * float(np.log2(np.e)), NEG_BIG)
    t1 = jnp.pad(t1, ((0, 0), (NA_BIAS_PAD, NA_BIAS_PAD), (0, 0), (0, 0)))
    t2 = jnp.concatenate([t1[:, 1:], t1[:, :-1]], axis=-1)
    return pl.pallas_call(
        _na_bias_kernel,
        name="na_bias",
        grid=(NA_HEADS, 3),
        in_specs=[pl.BlockSpec((1,) + t2.shape[1:], lambda h, k: (h, 0, 0, 0))],
        out_specs=pl.BlockSpec((1, 1, NA_TK, NA_TQ), lambda h, k: (k, h, 0, 0)),
        out_shape=jax.ShapeDtypeStruct((3, NA_HEADS, NA_TK, NA_TQ), F32),
        compiler_params=pltpu.CompilerParams(dimension_semantics=("parallel", "arbitrary")),
    )(t2)


NA_BIAS_PAD = NA_QROWS
NA_BLOCK_OFF = (0, NA_KH // 2, NA_WROWS - NA_QROWS)


def _na_row_ok(kind, a, jw):
    first_key = (0, a, NA_WROWS - NA_KH)[kind]
    return first_key <= jw < first_key + NA_KH


def _na_bias_kernel(t2_ref, o_ref):
    kind = pl.program_id(1)
    lane = lax.broadcasted_iota(jnp.int32, (GRID_W, 2 * GRID_W), 1)
    for k, off in enumerate(NA_BLOCK_OFF):
        @pl.when(kind == k)
        def _(k=k, off=off):
            for jw in range(NA_WROWS):
                for a in range(0, NA_QROWS, 2):
                    tile = t2_ref[0, jw - off - a + NA_KH - 1 + NA_BIAS_PAD - 1]
                    ok_a, ok_b = _na_row_ok(k, a, jw), _na_row_ok(k, a + 1, jw)
                    if ok_a and ok_b:
                        val = tile
                    elif ok_a:
                        val = jnp.where(lane < GRID_W, tile, NEG_BIG)
                    elif ok_b:
                        val = jnp.where(lane >= GRID_W, tile, NEG_BIG)
                    else:
                        val = jnp.full(tile.shape, NEG_BIG, F32)
                    o_ref[0, 0, jw * GRID_W:(jw + 1) * GRID_W, a * GRID_W:(a + 2) * GRID_W] = val


def _na(proj, batch, seq_len, bias_tbl):
    t = proj.shape[1]
    rows = seq_len // GRID_W
    nblk = rows // NA_QROWS
    assert rows % (NA_QROWS * NA_NB) == 0 and nblk >= NA_WCHUNKS
    assert NA_NB >= NA_WCHUNKS - 1
    nstep = nblk // NA_NB
    tq = NA_NB * NA_TQ
    cb = lambda off: _slab(off) // HEAD_DIM
    return pl.pallas_call(
        functools.partial(_na_kernel, nblk=nblk),
        name="na",
        grid=(NA_HEADS, batch, nstep),
        in_specs=[
            pl.BlockSpec((None, tq, HEAD_DIM), lambda h, b, i: (cb(OFF_NA_Q) + h, b * nstep + i, 0)),
            pl.BlockSpec((None, seq_len, HEAD_DIM), lambda h, b, i: (cb(OFF_NA_K) + h, b, 0)),
            pl.BlockSpec((None, seq_len, HEAD_DIM), lambda h, b, i: (cb(OFF_NA_V) + h, b, 0)),
            pl.BlockSpec((None, tq, HEAD_DIM), lambda h, b, i: (cb(OFF_NA_Z) + h, b * nstep + i, 0)),
            pl.BlockSpec((3, 1, NA_TK, NA_TQ), lambda h, b, i: (0, h, 0, 0)),
        ],
        out_specs=pl.BlockSpec((None, tq, HEAD_DIM), lambda h, b, i: (h, b * nstep + i, 0)),
        out_shape=jax.ShapeDtypeStruct((NA_HEADS, t, HEAD_DIM), BF16),
        scratch_shapes=[
            pltpu.VMEM((nblk, HEAD_DIM + NA_ONES_ROWS, NA_TQ), BF16),
            pltpu.VMEM((NA_NB, NA_TK, NA_TQ), F32),
        ],
        compiler_params=pltpu.CompilerParams(
            dimension_semantics=("parallel", "parallel", "arbitrary"),
            vmem_limit_bytes=VMEM_LIMIT_BYTES),
    )(proj, proj, proj, proj, bias_tbl)


def _gqa_kernel(q_ref, k_ref, v_ref, z_ref, cos_ref, sin_ref, cost_ref, sint_ref, o_ref,
                qt_scr, k_scr, vt_scr, m_scr, acc_scr, sa_scr, sb_scr, *, nk):
    i = pl.program_id(2)
    quarter = HEAD_DIM // 4

    def scores(c, s_buf):
        start = pl.multiple_of(c * GQA_TK, GQA_TK)
        k = k_scr[pl.ds(start, GQA_TK), :]
        for g in range(GQA_GROUP):
            s_buf[g] = jnp.dot(k, qt_scr[g], preferred_element_type=F32)

    @pl.when(i == 0)
    def _():
        def fill(c, carry):
            start = pl.multiple_of(c * GQA_TK, GQA_TK)
            rows = pl.ds(start, GQA_TK)
            v = v_ref[rows, :].astype(F32)
            vt_scr[c, 0:HEAD_DIM, :] = v.T.astype(BF16)
            vt_scr[c, HEAD_DIM:, :] = jnp.ones((GQA_ONES_ROWS, GQA_TK), BF16)
            kk = k_ref[rows, :].astype(F32)
            lane = lax.broadcasted_iota(jnp.int32, kk.shape, 1)
            partner = jnp.where((lane % (2 * quarter)) < quarter,
                                pltpu.roll(kk, HEAD_DIM - quarter, 1),
                                pltpu.roll(kk, quarter, 1))
            k_scr[rows, :] = (kk * cos_ref[rows, :] + partner * sin_ref[rows, :]).astype(BF16)
            return carry
        lax.fori_loop(0, nk, fill, 0)

    for g in range(GQA_GROUP):
        qt = q_ref[g].astype(F32).T
        partner = jnp.concatenate([qt[quarter:2 * quarter], qt[0:quarter],
                                   qt[3 * quarter:], qt[2 * quarter:3 * quarter]], axis=0)
        qt_scr[g] = (qt * cost_ref[...] + partner * sint_ref[...]).astype(BF16)
    m_scr[...] = jnp.full(m_scr.shape, -jnp.inf, F32)
    acc_scr[...] = jnp.zeros(acc_scr.shape, F32)

    def step(c_cur, buf_cur, c_next=None, buf_next=None):
        vt = vt_scr[c_cur]
        if c_next is not None:
            start = pl.multiple_of(c_next * GQA_TK, GQA_TK)
            k = k_scr[pl.ds(start, GQA_TK), :]
        for g in range(GQA_GROUP):
            if c_next is not None:
                buf_next[g] = jnp.dot(k, qt_scr[g], preferred_element_type=F32)
            s = buf_cur[g]
            m_old = m_scr[g]
            m_new = jnp.maximum(m_old, jnp.max(s, axis=0, keepdims=True))
            alpha = jnp.exp2(EXP2_SCALE * (m_old - m_new))
            p = jnp.exp2(EXP2_SCALE * (s - m_new)).astype(BF16)
            acc_scr[g] = alpha * acc_scr[g] + jnp.dot(vt, p, preferred_element_type=F32)
            m_scr[g] = m_new

    scores(0, sa_scr)

    def pair(j, carry):
        step(2 * j, sa_scr, 2 * j + 1, sb_scr)
        step(2 * j + 1, sb_scr, 2 * j + 2, sa_scr)
        return carry

    lax.fori_loop(0, nk // 2 - 1, pair, 0)
    step(nk - 2, sa_scr, nk - 1, sb_scr)
    step(nk - 1, sb_scr)
    for g in range(GQA_GROUP):
        cols = slice(g * HEAD_DIM, (g + 1) * HEAD_DIM)
        acc = acc_scr[g]
        o = (acc[0:HEAD_DIM] / acc[HEAD_DIM:HEAD_DIM + 1]).T
        o_ref[:, cols] = (o * z_ref[g].astype(F32)).astype(o_ref.dtype)


def _gqa(proj, batch, seq_len, cos_t, sin_t, cos_tt, sin_tt):
    t = proj.shape[1]
    assert seq_len % GQA_TQ == 0 and seq_len % (2 * GQA_TK) == 0
    nq = seq_len // GQA_TQ
    gw = GQA_GROUP * HEAD_DIM
    return pl.pallas_call(
        functools.partial(_gqa_kernel, nk=seq_len // GQA_TK),
        name="gqa",
        grid=(batch, GQA_KV_HEADS, nq),
        in_specs=[
            pl.BlockSpec((GQA_GROUP, GQA_TQ, HEAD_DIM), lambda b, g, i: (_slab(OFF_GQ_Q) // gw + g, b * nq + i, 0)),
            pl.BlockSpec((None, seq_len, HEAD_DIM), lambda b, g, i: (_slab(OFF_GQ_K) // HEAD_DIM + g, b, 0)),
            pl.BlockSpec((None, seq_len, HEAD_DIM), lambda b, g, i: (_slab(OFF_GQ_V) // HEAD_DIM + g, b, 0)),
            pl.BlockSpec((GQA_GROUP, GQA_TQ, HEAD_DIM), lambda b, g, i: (_slab(OFF_GQ_Z) // gw + g, b * nq + i, 0)),
            pl.BlockSpec((seq_len, HEAD_DIM), lambda b, g, i: (0, 0), pipeline_mode=pl.Buffered(1)),
            pl.BlockSpec((seq_len, HEAD_DIM), lambda b, g, i: (0, 0), pipeline_mode=pl.Buffered(1)),
            pl.BlockSpec((HEAD_DIM, GQA_TQ), lambda b, g, i: (0, i)),
            pl.BlockSpec((HEAD_DIM, GQA_TQ), lambda b, g, i: (0, i)),
        ],
        out_specs=pl.BlockSpec((GQA_TQ, gw), lambda b, g, i: (b * nq + i, g)),
        out_shape=jax.ShapeDtypeStruct((t, GQ_W), BF16),
        scratch_shapes=[
            pltpu.VMEM((GQA_GROUP, HEAD_DIM, GQA_TQ), BF16),
            pltpu.VMEM((seq_len, HEAD_DIM), BF16),
            pltpu.VMEM((seq_len // GQA_TK, HEAD_DIM + GQA_ONES_ROWS, GQA_TK), BF16),
            pltpu.VMEM((GQA_GROUP, 1, GQA_TQ), F32),
            pltpu.VMEM((GQA_GROUP, HEAD_DIM + GQA_ONES_ROWS, GQA_TQ), F32),
            pltpu.VMEM((GQA_GROUP, GQA_TK, GQA_TQ), F32),
            pltpu.VMEM((GQA_GROUP, GQA_TK, GQA_TQ), F32),
        ],
        compiler_params=pltpu.CompilerParams(
            dimension_semantics=("parallel", "parallel", "arbitrary"),
            vmem_limit_bytes=VMEM_LIMIT_BYTES),
    )(proj, proj, proj, proj, cos_t, sin_t, cos_tt, sin_tt)


def _tail_kernel(x_ref, oa_ref, ob_ref, ga_ref, gb_ref, wa_ref, wb_ref, wo_ref, y_ref):
    oa = jnp.concatenate([oa_ref[h] for h in range(NA_HEADS)], axis=1)
    pa = jnp.dot(oa, wa_ref[...], preferred_element_type=F32)
    pb = jnp.dot(ob_ref[...], wb_ref[...], preferred_element_type=F32)
    merged = jnp.concatenate(
        [(ga_ref[c].astype(F32) * pa[:, c * HEAD_DIM:(c + 1) * HEAD_DIM]
          + gb_ref[c].astype(F32) * pb[:, c * HEAD_DIM:(c + 1) * HEAD_DIM]).astype(BF16)
         for c in range(D_MODEL // HEAD_DIM)], axis=1)
    y_ref[...] = x_ref[...] + jnp.dot(merged, wo_ref[...], preferred_element_type=F32)


def _tail(x2d, oa, ob, proj, wa_bf, wb_bf, wo_bf):
    t = x2d.shape[0]
    assert t % TAIL_TM == 0
    resident = dict(pipeline_mode=pl.Buffered(1))
    return pl.pallas_call(
        _tail_kernel,
        name="tail",
        grid=(t // TAIL_TM,),
        in_specs=[
            pl.BlockSpec((TAIL_TM, D_MODEL), lambda i: (i, 0)),
            pl.BlockSpec((NA_HEADS, TAIL_TM, HEAD_DIM), lambda i: (0, i, 0)),
            pl.BlockSpec((TAIL_TM, GQ_W), lambda i: (i, 0)),
            pl.BlockSpec((D_MODEL // HEAD_DIM, TAIL_TM, HEAD_DIM), lambda i: (_slab(OFF_GA) // D_MODEL, i, 0)),
            pl.BlockSpec((D_MODEL // HEAD_DIM, TAIL_TM, HEAD_DIM), lambda i: (_slab(OFF_GB) // D_MODEL, i, 0)),
            pl.BlockSpec((NA_W, D_MODEL), lambda i: (0, 0), **resident),
            pl.BlockSpec((GQ_W, D_MODEL), lambda i: (0, 0), **resident),
            pl.BlockSpec((D_MODEL, D_MODEL), lambda i: (0, 0), **resident),
        ],
        out_specs=pl.BlockSpec((TAIL_TM, D_MODEL), lambda i: (i, 0)),
        out_shape=jax.ShapeDtypeStruct((t, D_MODEL), F32),
        compiler_params=pltpu.CompilerParams(
            dimension_semantics=("parallel",),
            vmem_limit_bytes=VMEM_LIMIT_BYTES),
    )(x2d, oa, ob, proj, proj, wa_bf, wb_bf, wo_bf)


def _rope_tables(seq_len):
    pos = np.arange(seq_len)
    axis = HEAD_DIM // 2
    freqs = ROPE_THETA ** (-np.arange(0, axis, 2, dtype=np.float64) / axis)
    ang_r = (pos // GRID_W)[:, None] * freqs[None, :]
    ang_c = (pos % GRID_W)[:, None] * freqs[None, :]
    cos_t = np.concatenate([np.cos(ang_r), np.cos(ang_r), np.cos(ang_c), np.cos(ang_c)], axis=-1)
    sin_t = np.concatenate([-np.sin(ang_r), np.sin(ang_r), -np.sin(ang_c), np.sin(ang_c)], axis=-1)
    tables = (cos_t, sin_t, cos_t.T, sin_t.T)
    return tuple(jnp.asarray(np.ascontiguousarray(tbl), dtype=F32) for tbl in tables)


def _layer(x, norm_g, w_in_bf, head_g, bias_tbl, wa_bf, wb_bf, gate_bias, wo_bf):
    batch, seq_len, _ = x.shape
    x2d = x.reshape(batch * seq_len, D_MODEL)
    proj = _proj(x2d, norm_g, w_in_bf, head_g, gate_bias)
    oa = _na(proj, batch, seq_len, bias_tbl)
    ob = _gqa(proj, batch, seq_len, *_rope_tables(seq_len))
    y = _tail(x2d, oa, ob, proj, wa_bf, wb_bf, wo_bf)
    return y.reshape(batch, seq_len, D_MODEL)


def kernel(x_prompt, x_sample, norm_g, w_in, na_q_g, na_k_g, na_rpb, gq_q_g, gq_k_g,
           w_branch_a, w_branch_b, gate_bias, w_out):
    depth = norm_g.shape[0]
    y_prompt, y_sample = x_prompt, x_sample
    for l in range(depth):
        head_g = jnp.stack([na_q_g[l], na_k_g[l], gq_q_g[l], gq_k_g[l]]).astype(F32)
        params = (norm_g[l][None, :].astype(F32), w_in[l].astype(BF16), head_g,
                  _na_bias_table(na_rpb[l]), w_branch_a[l].astype(BF16), w_branch_b[l].astype(BF16),
                  gate_bias[l][None, :].astype(F32), w_out[l].astype(BF16))
        y_prompt = _layer(y_prompt, *params)
        y_sample = _layer(y_sample, *params)
    return (y_prompt, y_sample)
```

```python
import functools

import numpy as np
import jax
import jax.numpy as jnp
from jax import lax
from jax.experimental import pallas as pl
from jax.experimental.pallas import tpu as pltpu

F32 = jnp.float32
BF16 = jnp.bfloat16

D_MODEL = 2048
HEAD_DIM = 128
GRID_W = 64
NA_HEADS = 8
NA_KH = 8
NA_KW = 16
GQA_HEADS = 8
GQA_KV_HEADS = 2
GQA_GROUP = GQA_HEADS // GQA_KV_HEADS
ROPE_THETA = 10000.0
EPS = 1e-6
SCALE = HEAD_DIM ** -0.5

NA_W = NA_HEADS * HEAD_DIM
GQ_W = GQA_HEADS * HEAD_DIM
KV_W = GQA_KV_HEADS * HEAD_DIM
OFF_NA_Q = 0
OFF_NA_K = OFF_NA_Q + NA_W
OFF_NA_V = OFF_NA_K + NA_W
OFF_NA_Z = OFF_NA_V + NA_W
OFF_GQ_Q = OFF_NA_Z + NA_W
OFF_GQ_K = OFF_GQ_Q + GQ_W
OFF_GQ_V = OFF_GQ_K + KV_W
OFF_GQ_Z = OFF_GQ_V + KV_W
OFF_GA = OFF_GQ_Z + GQ_W
OFF_GB = OFF_GA + D_MODEL
IN_WIDTH = OFF_GB + D_MODEL


def _slab(off):
    return (off + 2 * D_MODEL) % IN_WIDTH

VMEM_LIMIT_BYTES = 56 * 1024 * 1024

PROJ_TM = 1024
PROJ_TN = 1536
PROJ_PAIR = 256
PROJ_WBLK = 512
NA_QROWS = 4
NA_TQ = NA_QROWS * GRID_W
NA_WCHUNKS = 3
NA_WROWS = NA_WCHUNKS * NA_QROWS
NA_TK = NA_WROWS * GRID_W
NA_NB_MAX = 16
NA_ONES_ROWS = 16
GQA_TQ = 512
GQA_TK = 512
GQA_ONES_ROWS = 16
EXP2_SCALE = SCALE * float(np.log2(np.e))
TAIL_TM = 256
NEG_BIG = -1e30


def _sigmoid(x):
    return 0.5 * jnp.tanh(0.5 * x) + 0.5


def _proj_kernel(x_ref, ng_ref, *rest):
    n_w = PROJ_TN // PROJ_WBLK
    w_refs, (hg_ref, gb_ref, o_ref, h_scr) = rest[:n_w], rest[n_w:]
    j = pl.program_id(1)

    @pl.when(j == 0)
    def _():
        x = x_ref[...]
        ms = jnp.mean(x * x, axis=-1, keepdims=True)
        h_scr[...] = (x * lax.rsqrt(ms + EPS) * ng_ref[...]).astype(BF16)

    def epilogue(col, a):
        kind, arg = _slab_head_kind(col)
        if kind == "gate":
            return _sigmoid(a + gb_ref[:, arg:arg + HEAD_DIM])
        if kind == "norm":
            ms = jnp.mean(a * a, axis=-1, keepdims=True)
            return a * lax.rsqrt(ms + EPS) * hg_ref[arg:arg + 1, :]
        if kind == "silu":
            return a * _sigmoid(a)
        return a

    for jj in range(IN_WIDTH // PROJ_TN):
        @pl.when(j == jj)
        def _(jj=jj):
            for pp in range(PROJ_TN // PROJ_PAIR):
                c0 = pp * PROJ_PAIR
                w_pair = w_refs[c0 // PROJ_WBLK][:, c0 % PROJ_WBLK:c0 % PROJ_WBLK + PROJ_PAIR]
                acc = jnp.dot(h_scr[...], w_pair.astype(BF16), preferred_element_type=F32)
                for half in range(PROJ_PAIR // HEAD_DIM):
                    c = pp * PROJ_PAIR + half * HEAD_DIM
                    val = epilogue(jj * PROJ_TN + c, acc[:, half * HEAD_DIM:(half + 1) * HEAD_DIM])
                    o_ref[c // HEAD_DIM] = val.astype(o_ref.dtype)


def _slab_head_kind(col):
    groups = ((OFF_GA, 2 * D_MODEL, "gate", None), (OFF_NA_Q, NA_W, "norm", 0), (OFF_NA_K, NA_W, "norm", 1),
              (OFF_NA_V, NA_W, "copy", None), (OFF_NA_Z, NA_W, "silu", None), (OFF_GQ_Q, GQ_W, "norm", 2),
              (OFF_GQ_K, KV_W, "norm", 3), (OFF_GQ_V, KV_W, "copy", None), (OFF_GQ_Z, GQ_W, "silu", None))
    for off, width, kind, arg in groups:
        if _slab(off) <= col < _slab(off) + width:
            return kind, (col - _slab(off)) if kind == "gate" else arg
    raise ValueError(col)


def _proj(x2d, norm_g, w_in, head_g, gate_bias):
    t = x2d.shape[0]
    tm = PROJ_TM
    assert t % tm == 0 and IN_WIDTH % PROJ_TN == 0 and PROJ_TN % PROJ_WBLK == 0
    assert PROJ_WBLK % PROJ_PAIR == 0 and OFF_GA % PROJ_WBLK == 0
    n_w = PROJ_TN // PROJ_WBLK
    n_wblk = IN_WIDTH // PROJ_WBLK

    def w_spec(u):
        return pl.BlockSpec((D_MODEL, PROJ_WBLK),
                            lambda i, j: (0, (n_w * j + u + OFF_GA // PROJ_WBLK) % n_wblk))

    return pl.pallas_call(
        _proj_kernel,
        name="proj",
        grid=(t // tm, IN_WIDTH // PROJ_TN),
        in_specs=[
            pl.BlockSpec((tm, D_MODEL), lambda i, j: (i, 0)),
            pl.BlockSpec((1, D_MODEL), lambda i, j: (0, 0)),
            *[w_spec(u) for u in range(n_w)],
            pl.BlockSpec((4, HEAD_DIM), lambda i, j: (0, 0)),
            pl.BlockSpec((1, 2 * D_MODEL), lambda i, j: (0, 0)),
        ],
        out_specs=pl.BlockSpec((PROJ_TN // HEAD_DIM, tm, HEAD_DIM), lambda i, j: (j, i, 0)),
        out_shape=jax.ShapeDtypeStruct((IN_WIDTH // HEAD_DIM, t, HEAD_DIM), BF16),
        scratch_shapes=[pltpu.VMEM((tm, D_MODEL), BF16)],
        compiler_params=pltpu.CompilerParams(
            dimension_semantics=("parallel", "arbitrary"),
            vmem_limit_bytes=VMEM_LIMIT_BYTES),
    )(x2d, norm_g, *([w_in] * n_w), head_g, gate_bias)


def _na_kernel(q_ref, k_ref, v_ref, z_ref, bias_ref, o_ref, vt_scr, s_scr, *, nblk):
    i = pl.program_id(2)
    nb_step = s_scr.shape[0]

    def transpose_values(c):
        rows = pl.ds(pl.multiple_of(c * NA_TQ, NA_TQ), NA_TQ)
        vt_scr[c, 0:HEAD_DIM, :] = v_ref[rows, :].astype(F32).T.astype(BF16)
        vt_scr[c, HEAD_DIM:, :] = jnp.ones((NA_ONES_ROWS, NA_TQ), BF16)

    @pl.when(i == 0)
    def _():
        transpose_values(0)

    for jb in range(nb_step):
        transpose_values(jnp.minimum(i * nb_step + jb + 1, nblk - 1))

    def first_chunk(nb):
        return jnp.clip(nb - 1, 0, nblk - NA_WCHUNKS)

    def block_scores(jb):
        start = pl.multiple_of(first_chunk(i * nb_step + jb) * NA_TQ, NA_TQ)
        k = k_ref[pl.ds(start, NA_TK), :]
        q = q_ref[jb * NA_TQ:(jb + 1) * NA_TQ, :]
        s_scr[jb] = lax.dot_general(k, q, (((1,), (1,)), ((), ())), preferred_element_type=F32)

    for jb in range(nb_step):
        block_scores(jb)

    for jb in range(nb_step):
        nb = i * nb_step + jb
        chunk0 = first_chunk(nb)
        kind = jnp.where(nb == 0, 0, jnp.where(nb == nblk - 1, 2, 1))
        logits = EXP2_SCALE * s_scr[jb] + bias_ref[kind, 0]
        m = jnp.max(logits, axis=0, keepdims=True)
        p = jnp.exp2(logits - m).astype(BF16)
        pv = jnp.dot(vt_scr[chunk0], p[0:NA_TQ], preferred_element_type=F32)
        for w in range(1, NA_WCHUNKS):
            pv += jnp.dot(vt_scr[chunk0 + w], p[w * NA_TQ:(w + 1) * NA_TQ],
                          preferred_element_type=F32)
        o = (pv[0:HEAD_DIM] / pv[HEAD_DIM:HEAD_DIM + 1]).T
        rows = slice(jb * NA_TQ, (jb + 1) * NA_TQ)
        o_ref[rows, :] = (o * z_ref[rows, :].astype(F32)).astype(o_ref.dtype)


def _na_bias_table(rpb):
    c = np.arange(GRID_W)[:, None]
    kc = np.arange(GRID_W)[None, :]
    cs = np.clip(c - NA_KW // 2, 0, GRID_W - NA_KW)
    col_ok = (kc >= cs) & (kc < cs + NA_KW)
    dc = kc - c + NA_KW - 1
    onehot = ((dc.T[None] == np.arange(2 * NA_KW - 1)[:, None, None]) & col_ok.T[None]).astype(np.float32)
    t1 = jnp.einsum('hrd,dkc->hrkc', rpb.astype(F32), onehot, precision=lax.Precision.HIGHEST)
    t1 = jnp.where(col_ok.T[None, None], t1 * float(np.log2(np.e)), NEG_BIG)
    t1 = jnp.pad(t1, ((0, 0), (NA_BIAS_PAD, NA_BIAS_PAD), (0, 0), (0, 0)))
    t2 = jnp.concatenate([t1[:, 1:], t1[:, :-1]], axis=-1)
    return pl.pallas_call(
        _na_bias_kernel,
        name="na_bias",
        grid=(NA_HEADS, 3),
        in_specs=[pl.BlockSpec((1,) + t2.shape[1:], lambda h, k: (h, 0, 0, 0))],
        out_specs=pl.BlockSpec((1, 1, NA_TK, NA_TQ), lambda h, k: (k, h, 0, 0)),
        out_shape=jax.ShapeDtypeStruct((3, NA_HEADS, NA_TK, NA_TQ), F32),
        compiler_params=pltpu.CompilerParams(dimension_semantics=("parallel", "arbitrary")),
    )(t2)


NA_BIAS_PAD = NA_QROWS
NA_BLOCK_OFF = (0, NA_KH // 2, NA_WROWS - NA_QROWS)


def _na_row_ok(kind, a, jw):
    first_key = (0, a, NA_WROWS - NA_KH)[kind]
    return first_key <= jw < first_key + NA_KH


def _na_bias_kernel(t2_ref, o_ref):
    kind = pl.program_id(1)
    lane = lax.broadcasted_iota(jnp.int32, (GRID_W, 2 * GRID_W), 1)
    for k, off in enumerate(NA_BLOCK_OFF):
        @pl.when(kind == k)
        def _(k=k, off=off):
            for jw in range(NA_WROWS):
                for a in range(0, NA_QROWS, 2):
                    tile = t2_ref[0, jw - off - a + NA_KH - 1 + NA_BIAS_PAD - 1]
                    ok_a, ok_b = _na_row_ok(k, a, jw), _na_row_ok(k, a + 1, jw)
                    if ok_a and ok_b:
                        val = tile
                    elif ok_a:
                        val = jnp.where(lane < GRID_W, tile, NEG_BIG)
                    elif ok_b:
                        val = jnp.where(lane >= GRID_W, tile, NEG_BIG)
                    else:
                        val = jnp.full(tile.shape, NEG_BIG, F32)
                    o_ref[0, 0, jw * GRID_W:(jw + 1) * GRID_W, a * GRID_W:(a + 2) * GRID_W] = val


def _na(proj, batch, seq_len, bias_tbl):
    t = proj.shape[1]
    rows = seq_len // GRID_W
    nblk = rows // NA_QROWS
    nb_step = min(NA_NB_MAX, nblk)
    assert nblk % nb_step == 0 and nblk >= NA_WCHUNKS
    assert nb_step >= NA_WCHUNKS - 1
    nstep = nblk // nb_step
    tq = nb_step * NA_TQ
    cb = lambda off: _slab(off) // HEAD_DIM
    return pl.pallas_call(
        functools.partial(_na_kernel, nblk=nblk),
        name="na",
        grid=(NA_HEADS, batch, nstep),
        in_specs=[
            pl.BlockSpec((None, tq, HEAD_DIM), lambda h, b, i: (cb(OFF_NA_Q) + h, b * nstep + i, 0)),
            pl.BlockSpec((None, seq_len, HEAD_DIM), lambda h, b, i: (cb(OFF_NA_K) + h, b, 0)),
            pl.BlockSpec((None, seq_len, HEAD_DIM), lambda h, b, i: (cb(OFF_NA_V) + h, b, 0)),
            pl.BlockSpec((None, tq, HEAD_DIM), lambda h, b, i: (cb(OFF_NA_Z) + h, b * nstep + i, 0)),
            pl.BlockSpec((3, 1, NA_TK, NA_TQ), lambda h, b, i: (0, h, 0, 0)),
        ],
        out_specs=pl.BlockSpec((None, tq, HEAD_DIM), lambda h, b, i: (h, b * nstep + i, 0)),
        out_shape=jax.ShapeDtypeStruct((NA_HEADS, t, HEAD_DIM), BF16),
        scratch_shapes=[
            pltpu.VMEM((nblk, HEAD_DIM + NA_ONES_ROWS, NA_TQ), BF16),
            pltpu.VMEM((nb_step, NA_TK, NA_TQ), F32),
        ],
        compiler_params=pltpu.CompilerParams(
            dimension_semantics=("parallel", "parallel", "arbitrary"),
            vmem_limit_bytes=VMEM_LIMIT_BYTES),
    )(proj, proj, proj, proj, bias_tbl)


def _gqa_kernel(q_ref, k_ref, v_ref, z_ref, cos_ref, sin_ref, cost_ref, sint_ref, o_ref,
                qt_scr, k_scr, vt_scr, m_scr, acc_scr, sa_scr, sb_scr, *, nk):
    i = pl.program_id(2)
    quarter = HEAD_DIM // 4

    def scores(c, s_buf):
        start = pl.multiple_of(c * GQA_TK, GQA_TK)
        k = k_scr[pl.ds(start, GQA_TK), :]
        for g in range(GQA_GROUP):
            s_buf[g] = jnp.dot(k, qt_scr[g], preferred_element_type=F32)

    @pl.when(i == 0)
    def _():
        def fill(c, carry):
            start = pl.multiple_of(c * GQA_TK, GQA_TK)
            rows = pl.ds(start, GQA_TK)
            v = v_ref[rows, :].astype(F32)
            vt_scr[c, 0:HEAD_DIM, :] = v.T.astype(BF16)
            vt_scr[c, HEAD_DIM:, :] = jnp.ones((GQA_ONES_ROWS, GQA_TK), BF16)
            kk = k_ref[rows, :].astype(F32)
            lane = lax.broadcasted_iota(jnp.int32, kk.shape, 1)
            partner = jnp.where((lane % (2 * quarter)) < quarter,
                                pltpu.roll(kk, HEAD_DIM - quarter, 1),
                                pltpu.roll(kk, quarter, 1))
            k_scr[rows, :] = (kk * cos_ref[rows, :] + partner * sin_ref[rows, :]).astype(BF16)
            return carry
        lax.fori_loop(0, nk, fill, 0)

    for g in range(GQA_GROUP):
        qt = q_ref[g].astype(F32).T
        partner = jnp.concatenate([qt[quarter:2 * quarter], qt[0:quarter],
                                   qt[3 * quarter:], qt[2 * quarter:3 * quarter]], axis=0)
        qt_scr[g] = (qt * cost_ref[...] + partner * sint_ref[...]).astype(BF16)
    m_scr[...] = jnp.full(m_scr.shape, -jnp.inf, F32)
    acc_scr[...] = jnp.zeros(acc_scr.shape, F32)

    def step(c_cur, buf_cur, c_next=None, buf_next=None):
        vt = vt_scr[c_cur]
        if c_next is not None:
            start = pl.multiple_of(c_next * GQA_TK, GQA_TK)
            k = k_scr[pl.ds(start, GQA_TK), :]
        for g in range(GQA_GROUP):
            if c_next is not None:
                buf_next[g] = jnp.dot(k, qt_scr[g], preferred_element_type=F32)
            s = buf_cur[g]
            m_old = m_scr[g]
            m_new = jnp.maximum(m_old, jnp.max(s, axis=0, keepdims=True))
            alpha = jnp.exp2(EXP2_SCALE * (m_old - m_new))
            p = jnp.exp2(EXP2_SCALE * (s - m_new)).astype(BF16)
            acc_scr[g] = alpha * acc_scr[g] + jnp.dot(vt, p, preferred_element_type=F32)
            m_scr[g] = m_new

    scores(0, sa_scr)

    def pair(j, carry):
        step(2 * j, sa_scr, 2 * j + 1, sb_scr)
        step(2 * j + 1, sb_scr, 2 * j + 2, sa_scr)
        return carry

    lax.fori_loop(0, nk // 2 - 1, pair, 0)
    step(nk - 2, sa_scr, nk - 1, sb_scr)
    step(nk - 1, sb_scr)
    for g in range(GQA_GROUP):
        cols = slice(g * HEAD_DIM, (g + 1) * HEAD_DIM)
        acc = acc_scr[g]
        o = (acc[0:HEAD_DIM] / acc[HEAD_DIM:HEAD_DIM + 1]).T
        o_ref[:, cols] = (o * z_ref[g].astype(F32)).astype(o_ref.dtype)


def _gqa(proj, batch, seq_len, cos_t, sin_t, cos_tt, sin_tt):
    t = proj.shape[1]
    assert seq_len % GQA_TQ == 0 and seq_len % (2 * GQA_TK) == 0
    nq = seq_len // GQA_TQ
    gw = GQA_GROUP * HEAD_DIM
    return pl.pallas_call(
        functools.partial(_gqa_kernel, nk=seq_len // GQA_TK),
        name="gqa",
        grid=(batch, GQA_KV_HEADS, nq),
        in_specs=[
            pl.BlockSpec((GQA_GROUP, GQA_TQ, HEAD_DIM), lambda b, g, i: (_slab(OFF_GQ_Q) // gw + g, b * nq + i, 0)),
            pl.BlockSpec((None, seq_len, HEAD_DIM), lambda b, g, i: (_slab(OFF_GQ_K) // HEAD_DIM + g, b, 0)),
            pl.BlockSpec((None, seq_len, HEAD_DIM), lambda b, g, i: (_slab(OFF_GQ_V) // HEAD_DIM + g, b, 0)),
            pl.BlockSpec((GQA_GROUP, GQA_TQ, HEAD_DIM), lambda b, g, i: (_slab(OFF_GQ_Z) // gw + g, b * nq + i, 0)),
            pl.BlockSpec((seq_len, HEAD_DIM), lambda b, g, i: (0, 0), pipeline_mode=pl.Buffered(1)),
            pl.BlockSpec((seq_len, HEAD_DIM), lambda b, g, i: (0, 0), pipeline_mode=pl.Buffered(1)),
            pl.BlockSpec((HEAD_DIM, GQA_TQ), lambda b, g, i: (0, i)),
            pl.BlockSpec((HEAD_DIM, GQA_TQ), lambda b, g, i: (0, i)),
        ],
        out_specs=pl.BlockSpec((GQA_TQ, gw), lambda b, g, i: (b * nq + i, g)),
        out_shape=jax.ShapeDtypeStruct((t, GQ_W), BF16),
        scratch_shapes=[
            pltpu.VMEM((GQA_GROUP, HEAD_DIM, GQA_TQ), BF16),
            pltpu.VMEM((seq_len, HEAD_DIM), BF16),
            pltpu.VMEM((seq_len // GQA_TK, HEAD_DIM + GQA_ONES_ROWS, GQA_TK), BF16),
            pltpu.VMEM((GQA_GROUP, 1, GQA_TQ), F32),
            pltpu.VMEM((GQA_GROUP, HEAD_DIM + GQA_ONES_ROWS, GQA_TQ), F32),
            pltpu.VMEM((GQA_GROUP, GQA_TK, GQA_TQ), F32),
            pltpu.VMEM((GQA_GROUP, GQA_TK, GQA_TQ), F32),
        ],
        compiler_params=pltpu.CompilerParams(
            dimension_semantics=("parallel", "parallel", "arbitrary"),
            vmem_limit_bytes=VMEM_LIMIT_BYTES),
    )(proj, proj, proj, proj, cos_t, sin_t, cos_tt, sin_tt)


def _tail_kernel(x_ref, oa_ref, ob_ref, ga_ref, gb_ref, wa_ref, wb_ref, wo_ref, y_ref):
    oa = jnp.concatenate([oa_ref[h] for h in range(NA_HEADS)], axis=1)
    pa = jnp.dot(oa, wa_ref[...], preferred_element_type=F32)
    pb = jnp.dot(ob_ref[...], wb_ref[...], preferred_element_type=F32)
    merged = jnp.concatenate(
        [(ga_ref[c].astype(F32) * pa[:, c * HEAD_DIM:(c + 1) * HEAD_DIM]
          + gb_ref[c].astype(F32) * pb[:, c * HEAD_DIM:(c + 1) * HEAD_DIM]).astype(BF16)
         for c in range(D_MODEL // HEAD_DIM)], axis=1)
    y_ref[...] = x_ref[...] + jnp.dot(merged, wo_ref[...], preferred_element_type=F32)


def _tail(x2d, oa, ob, proj, wa_bf, wb_bf, wo_bf):
    t = x2d.shape[0]
    assert t % TAIL_TM == 0
    resident = dict(pipeline_mode=pl.Buffered(1))
    return pl.pallas_call(
        _tail_kernel,
        name="tail",
        grid=(t // TAIL_TM,),
        in_specs=[
            pl.BlockSpec((TAIL_TM, D_MODEL), lambda i: (i, 0)),
            pl.BlockSpec((NA_HEADS, TAIL_TM, HEAD_DIM), lambda i: (0, i, 0)),
            pl.BlockSpec((TAIL_TM, GQ_W), lambda i: (i, 0)),
            pl.BlockSpec((D_MODEL // HEAD_DIM, TAIL_TM, HEAD_DIM), lambda i: (_slab(OFF_GA) // D_MODEL, i, 0)),
            pl.BlockSpec((D_MODEL // HEAD_DIM, TAIL_TM, HEAD_DIM), lambda i: (_slab(OFF_GB) // D_MODEL, i, 0)),
            pl.BlockSpec((NA_W, D_MODEL), lambda i: (0, 0), **resident),
            pl.BlockSpec((GQ_W, D_MODEL), lambda i: (0, 0), **resident),
            pl.BlockSpec((D_MODEL, D_MODEL), lambda i: (0, 0), **resident),
        ],
        out_specs=pl.BlockSpec((TAIL_TM, D_MODEL), lambda i: (i, 0)),
        out_shape=jax.ShapeDtypeStruct((t, D_MODEL), F32),
        compiler_params=pltpu.CompilerParams(
            dimension_semantics=("parallel",),
            vmem_limit_bytes=VMEM_LIMIT_BYTES),
    )(x2d, oa, ob, proj, proj, wa_bf, wb_bf, wo_bf)


def _rope_tables(seq_len):
    pos = np.arange(seq_len)
    axis = HEAD_DIM // 2
    freqs = ROPE_THETA ** (-np.arange(0, axis, 2, dtype=np.float64) / axis)
    ang_r = (pos // GRID_W)[:, None] * freqs[None, :]
    ang_c = (pos % GRID_W)[:, None] * freqs[None, :]
    cos_t = np.concatenate([np.cos(ang_r), np.cos(ang_r), np.cos(ang_c), np.cos(ang_c)], axis=-1)
    sin_t = np.concatenate([-np.sin(ang_r), np.sin(ang_r), -np.sin(ang_c), np.sin(ang_c)], axis=-1)
    tables = (cos_t, sin_t, cos_t.T, sin_t.T)
    return tuple(jnp.asarray(np.ascontiguousarray(tbl), dtype=F32) for tbl in tables)


def _layer(x, norm_g, w_in, head_g, bias_tbl, wa_bf, wb_bf, gate_bias, wo_bf):
    batch, seq_len, _ = x.shape
    x2d = x.reshape(batch * seq_len, D_MODEL)
    proj = _proj(x2d, norm_g, w_in, head_g, gate_bias)
    oa = _na(proj, batch, seq_len, bias_tbl)
    ob = _gqa(proj, batch, seq_len, *_rope_tables(seq_len))
    y = _tail(x2d, oa, ob, proj, wa_bf, wb_bf, wo_bf)
    return y.reshape(batch, seq_len, D_MODEL)


def kernel(x_prompt, x_sample, norm_g, w_in, na_q_g, na_k_g, na_rpb, gq_q_g, gq_k_g,
           w_branch_a, w_branch_b, gate_bias, w_out):
    depth = norm_g.shape[0]
    y_prompt, y_sample = x_prompt, x_sample
    for l in range(depth):
        head_g = jnp.stack([na_q_g[l], na_k_g[l], gq_q_g[l], gq_k_g[l]]).astype(F32)
        params = (norm_g[l][None, :].astype(F32), w_in[l], head_g,
                  _na_bias_table(na_rpb[l]), w_branch_a[l].astype(BF16), w_branch_b[l].astype(BF16),
                  gate_bias[l][None, :].astype(F32), w_out[l].astype(BF16))
        y_prompt = _layer(y_prompt, *params)
        y_sample = _layer(y_sample, *params)
    return (y_prompt, y_sample)
```

```python
import functools

import numpy as np
import jax
import jax.numpy as jnp
from jax import lax
from jax.experimental import pallas as pl
from jax.experimental.pallas import tpu as pltpu

F32 = jnp.float32
BF16 = jnp.bfloat16

D_MODEL = 2048
HEAD_DIM = 128
GRID_W = 64
NA_HEADS = 8
NA_KH = 8
NA_KW = 16
GQA_HEADS = 8
GQA_KV_HEADS = 2
GQA_GROUP = GQA_HEADS // GQA_KV_HEADS
ROPE_THETA = 10000.0
EPS = 1e-6
SCALE = HEAD_DIM ** -0.5

NA_W = NA_HEADS * HEAD_DIM
GQ_W = GQA_HEADS * HEAD_DIM
KV_W = GQA_KV_HEADS * HEAD_DIM
OFF_NA_Q = 0
OFF_NA_K = OFF_NA_Q + NA_W
OFF_NA_V = OFF_NA_K + NA_W
OFF_NA_Z = OFF_NA_V + NA_W
OFF_GQ_Q = OFF_NA_Z + NA_W
OFF_GQ_K = OFF_GQ_Q + GQ_W
OFF_GQ_V = OFF_GQ_K + KV_W
OFF_GQ_Z = OFF_GQ_V + KV_W
OFF_GA = OFF_GQ_Z + GQ_W
OFF_GB = OFF_GA + D_MODEL
IN_WIDTH = OFF_GB + D_MODEL


def _slab(off):
    return (off + 2 * D_MODEL) % IN_WIDTH

VMEM_LIMIT_BYTES = 56 * 1024 * 1024

PROJ_TM = 1024
PROJ_TN = 1536
PROJ_PAIR = 256
PROJ_WBLK = 512
NA_QROWS = 4
NA_TQ = NA_QROWS * GRID_W
NA_WCHUNKS = 3
NA_WROWS = NA_WCHUNKS * NA_QROWS
NA_TK = NA_WROWS * GRID_W
NA_NB_MAX = 16
NA_ONES_ROWS = 16
GQA_TQ = 512
GQA_TK = 512
GQA_ONES_ROWS = 16
EXP2_SCALE = SCALE * float(np.log2(np.e))
TAIL_TM = 256
NEG_BIG = -1e30


def _sigmoid(x):
    return 0.5 * jnp.tanh(0.5 * x) + 0.5


def _proj_kernel(x_ref, ng_ref, *rest):
    n_w = PROJ_TN // PROJ_WBLK
    w_refs, (hg_ref, gb_ref, o_ref, h_scr) = rest[:n_w], rest[n_w:]
    j = pl.program_id(1)

    @pl.when(j == 0)
    def _():
        x = x_ref[...]
        ms = jnp.mean(x * x, axis=-1, keepdims=True)
        h_scr[...] = (x * lax.rsqrt(ms + EPS) * ng_ref[...]).astype(BF16)

    def epilogue(col, a):
        kind, arg = _slab_head_kind(col)
        if kind == "gate":
            return _sigmoid(a + gb_ref[:, arg:arg + HEAD_DIM])
        if kind == "norm":
            ms = jnp.mean(a * a, axis=-1, keepdims=True)
            return a * lax.rsqrt(ms + EPS) * hg_ref[arg:arg + 1, :]
        if kind == "silu":
            return a * _sigmoid(a)
        return a

    for jj in range(IN_WIDTH // PROJ_TN):
        @pl.when(j == jj)
        def _(jj=jj):
            n_pairs = PROJ_TN // PROJ_PAIR
            tm = h_scr.shape[0]
            for pp in range(n_pairs):
                c0 = pp * PROJ_PAIR
                w_pair = w_refs[c0 // PROJ_WBLK][:, c0 % PROJ_WBLK:c0 % PROJ_WBLK + PROJ_PAIR]
                n_row_chunks = 2 if pp == n_pairs - 1 else 1
                for rc in range(n_row_chunks):
                    rows = slice(rc * tm // n_row_chunks, (rc + 1) * tm // n_row_chunks)
                    acc = jnp.dot(h_scr[rows, :], w_pair, preferred_element_type=F32)
                    for half in range(PROJ_PAIR // HEAD_DIM):
                        c = pp * PROJ_PAIR + half * HEAD_DIM
                        val = epilogue(jj * PROJ_TN + c, acc[:, half * HEAD_DIM:(half + 1) * HEAD_DIM])
                        o_ref[c // HEAD_DIM, rows, :] = val.astype(o_ref.dtype)


def _slab_head_kind(col):
    groups = ((OFF_GA, 2 * D_MODEL, "gate", None), (OFF_NA_Q, NA_W, "norm", 0), (OFF_NA_K, NA_W, "norm", 1),
              (OFF_NA_V, NA_W, "copy", None), (OFF_NA_Z, NA_W, "silu", None), (OFF_GQ_Q, GQ_W, "norm", 2),
              (OFF_GQ_K, KV_W, "norm", 3), (OFF_GQ_V, KV_W, "copy", None), (OFF_GQ_Z, GQ_W, "silu", None))
    for off, width, kind, arg in groups:
        if _slab(off) <= col < _slab(off) + width:
            return kind, (col - _slab(off)) if kind == "gate" else arg
    raise ValueError(col)


def _proj(x2d, norm_g, w_in, head_g, gate_bias):
    t = x2d.shape[0]
    tm = PROJ_TM
    assert t % tm == 0 and IN_WIDTH % PROJ_TN == 0 and PROJ_TN % PROJ_WBLK == 0
    assert PROJ_WBLK % PROJ_PAIR == 0 and OFF_GA % PROJ_WBLK == 0
    n_w = PROJ_TN // PROJ_WBLK
    n_wblk = IN_WIDTH // PROJ_WBLK

    def w_spec(u):
        return pl.BlockSpec((D_MODEL, PROJ_WBLK),
                            lambda i, j: (0, (n_w * j + u + OFF_GA // PROJ_WBLK) % n_wblk))

    return pl.pallas_call(
        _proj_kernel,
        name="proj",
        grid=(t // tm, IN_WIDTH // PROJ_TN),
        in_specs=[
            pl.BlockSpec((tm, D_MODEL), lambda i, j: (i, 0)),
            pl.BlockSpec((1, D_MODEL), lambda i, j: (0, 0)),
            *[w_spec(u) for u in range(n_w)],
            pl.BlockSpec((4, HEAD_DIM), lambda i, j: (0, 0)),
            pl.BlockSpec((1, 2 * D_MODEL), lambda i, j: (0, 0)),
        ],
        out_specs=pl.BlockSpec((PROJ_TN // HEAD_DIM, tm, HEAD_DIM), lambda i, j: (j, i, 0)),
        out_shape=jax.ShapeDtypeStruct((IN_WIDTH // HEAD_DIM, t, HEAD_DIM), BF16),
        scratch_shapes=[pltpu.VMEM((tm, D_MODEL), BF16)],
        compiler_params=pltpu.CompilerParams(
            dimension_semantics=("parallel", "arbitrary"),
            vmem_limit_bytes=VMEM_LIMIT_BYTES),
    )(x2d, norm_g, *([w_in] * n_w), head_g, gate_bias)


def _na_kernel(q_ref, k_ref, v_ref, z_ref, bias_ref, o_ref, vt_scr, s_scr, *, nblk):
    i = pl.program_id(2)
    nb_step = s_scr.shape[0]

    def transpose_values(c):
        rows = pl.ds(pl.multiple_of(c * NA_TQ, NA_TQ), NA_TQ)
        vt_scr[c, 0:HEAD_DIM, :] = v_ref[rows, :].astype(F32).T.astype(BF16)
        vt_scr[c, HEAD_DIM:, :] = jnp.ones((NA_ONES_ROWS, NA_TQ), BF16)

    @pl.when(i == 0)
    def _():
        transpose_values(0)

    for jb in range(nb_step):
        transpose_values(jnp.minimum(i * nb_step + jb + 1, nblk - 1))

    def first_chunk(nb):
        return jnp.clip(nb - 1, 0, nblk - NA_WCHUNKS)

    def block_scores(jb):
        start = pl.multiple_of(first_chunk(i * nb_step + jb) * NA_TQ, NA_TQ)
        k = k_ref[pl.ds(start, NA_TK), :]
        q = q_ref[jb * NA_TQ:(jb + 1) * NA_TQ, :]
        s_scr[jb] = lax.dot_general(k, q, (((1,), (1,)), ((), ())), preferred_element_type=F32)

    for jb in range(nb_step):
        block_scores(jb)

    for jb in range(nb_step):
        nb = i * nb_step + jb
        chunk0 = first_chunk(nb)
        kind = jnp.where(nb == 0, 0, jnp.where(nb == nblk - 1, 2, 1))
        logits = s_scr[jb] + bias_ref[kind, 0]
        m = jnp.max(logits, axis=0, keepdims=True)
        p = jnp.exp2(logits - m).astype(BF16)
        pv = jnp.dot(vt_scr[chunk0], p[0:NA_TQ], preferred_element_type=F32)
        for w in range(1, NA_WCHUNKS):
            pv += jnp.dot(vt_scr[chunk0 + w], p[w * NA_TQ:(w + 1) * NA_TQ],
                          preferred_element_type=F32)
        o = (pv[0:HEAD_DIM] / pv[HEAD_DIM:HEAD_DIM + 1]).T
        rows = slice(jb * NA_TQ, (jb + 1) * NA_TQ)
        o_ref[rows, :] = (o * z_ref[rows, :].astype(F32)).astype(o_ref.dtype)


def _na_bias_table(rpb):
    c = np.arange(GRID_W)[:, None]
    kc = np.arange(GRID_W)[None, :]
    cs = np.clip(c - NA_KW // 2, 0, GRID_W - NA_KW)
    col_ok = (kc >= cs) & (kc < cs + NA_KW)
    dc = kc - c + NA_KW - 1
    onehot = ((dc.T[None] == np.arange(2 * NA_KW - 1)[:, None, None]) & col_ok.T[None]).astype(np.float32)
    t1 = jnp.einsum('hrd,dkc->hrkc', rpb.astype(F32), onehot, precision=lax.Precision.HIGHEST)
    t1 = jnp.where(col_ok.T[None, None], t1 * float(np.log2(np.e)), NEG_BIG)
    t1 = jnp.pad(t1, ((0, 0), (NA_BIAS_PAD, NA_BIAS_PAD), (0, 0), (0, 0)))
    t2 = jnp.concatenate([t1[:, 1:], t1[:, :-1]], axis=-1)
    return pl.pallas_call(
        _na_bias_kernel,
        name="na_bias",
        grid=(NA_HEADS, 3),
        in_specs=[pl.BlockSpec((1,) + t2.shape[1:], lambda h, k: (h, 0, 0, 0))],
        out_specs=pl.BlockSpec((1, 1, NA_TK, NA_TQ), lambda h, k: (k, h, 0, 0)),
        out_shape=jax.ShapeDtypeStruct((3, NA_HEADS, NA_TK, NA_TQ), F32),
        compiler_params=pltpu.CompilerParams(dimension_semantics=("parallel", "arbitrary")),
    )(t2)


NA_BIAS_PAD = NA_QROWS
NA_BLOCK_OFF = (0, NA_KH // 2, NA_WROWS - NA_QROWS)


def _na_row_ok(kind, a, jw):
    first_key = (0, a, NA_WROWS - NA_KH)[kind]
    return first_key <= jw < first_key + NA_KH


def _na_bias_kernel(t2_ref, o_ref):
    kind = pl.program_id(1)
    lane = lax.broadcasted_iota(jnp.int32, (GRID_W, 2 * GRID_W), 1)
    for k, off in enumerate(NA_BLOCK_OFF):
        @pl.when(kind == k)
        def _(k=k, off=off):
            for jw in range(NA_WROWS):
                for a in range(0, NA_QROWS, 2):
                    tile = t2_ref[0, jw - off - a + NA_KH - 1 + NA_BIAS_PAD - 1]
                    ok_a, ok_b = _na_row_ok(k, a, jw), _na_row_ok(k, a + 1, jw)
                    if ok_a and ok_b:
                        val = tile
                    elif ok_a:
                        val = jnp.where(lane < GRID_W, tile, NEG_BIG)
                    elif ok_b:
                        val = jnp.where(lane >= GRID_W, tile, NEG_BIG)
                    else:
                        val = jnp.full(tile.shape, NEG_BIG, F32)
                    o_ref[0, 0, jw * GRID_W:(jw + 1) * GRID_W, a * GRID_W:(a + 2) * GRID_W] = val


def _na(proj, batch, seq_len, bias_tbl):
    t = proj.shape[1]
    rows = seq_len // GRID_W
    nblk = rows // NA_QROWS
    nb_step = min(NA_NB_MAX, nblk)
    assert nblk % nb_step == 0 and nblk >= NA_WCHUNKS
    assert nb_step >= NA_WCHUNKS - 1
    nstep = nblk // nb_step
    tq = nb_step * NA_TQ
    cb = lambda off: _slab(off) // HEAD_DIM
    return pl.pallas_call(
        functools.partial(_na_kernel, nblk=nblk),
        name="na",
        grid=(NA_HEADS, batch, nstep),
        in_specs=[
            pl.BlockSpec((None, tq, HEAD_DIM), lambda h, b, i: (cb(OFF_NA_Q) + h, b * nstep + i, 0)),
            pl.BlockSpec((None, seq_len, HEAD_DIM), lambda h, b, i: (cb(OFF_NA_K) + h, b, 0)),
            pl.BlockSpec((None, seq_len, HEAD_DIM), lambda h, b, i: (cb(OFF_NA_V) + h, b, 0)),
            pl.BlockSpec((None, tq, HEAD_DIM), lambda h, b, i: (cb(OFF_NA_Z) + h, b * nstep + i, 0)),
            pl.BlockSpec((3, 1, NA_TK, NA_TQ), lambda h, b, i: (0, h, 0, 0)),
        ],
        out_specs=pl.BlockSpec((None, tq, HEAD_DIM), lambda h, b, i: (h, b * nstep + i, 0)),
        out_shape=jax.ShapeDtypeStruct((NA_HEADS, t, HEAD_DIM), BF16),
        scratch_shapes=[
            pltpu.VMEM((nblk, HEAD_DIM + NA_ONES_ROWS, NA_TQ), BF16),
            pltpu.VMEM((nb_step, NA_TK, NA_TQ), F32),
        ],
        compiler_params=pltpu.CompilerParams(
            dimension_semantics=("parallel", "parallel", "arbitrary"),
            vmem_limit_bytes=VMEM_LIMIT_BYTES),
    )(proj, proj, proj, proj, bias_tbl)


def _gqa_kernel(q_ref, k_ref, v_ref, z_ref, cos_ref, sin_ref, cost_ref, sint_ref, o_ref,
                qt_scr, k_scr, vt_scr, m_scr, acc_scr, sa_scr, sb_scr, *, nk):
    i = pl.program_id(2)
    quarter = HEAD_DIM // 4

    def scores(c, s_buf):
        start = pl.multiple_of(c * GQA_TK, GQA_TK)
        k = k_scr[pl.ds(start, GQA_TK), :]
        for g in range(GQA_GROUP):
            s_buf[g] = jnp.dot(k, qt_scr[g], preferred_element_type=F32)

    @pl.when(i == 0)
    def _():
        def fill(c, carry):
            start = pl.multiple_of(c * GQA_TK, GQA_TK)
            rows = pl.ds(start, GQA_TK)
            v = v_ref[rows, :].astype(F32)
            vt_scr[c, 0:HEAD_DIM, :] = v.T.astype(BF16)
            vt_scr[c, HEAD_DIM:, :] = jnp.ones((GQA_ONES_ROWS, GQA_TK), BF16)
            kk = k_ref[rows, :].astype(F32)
            lane = lax.broadcasted_iota(jnp.int32, kk.shape, 1)
            partner = jnp.where((lane % (2 * quarter)) < quarter,
                                pltpu.roll(kk, HEAD_DIM - quarter, 1),
                                pltpu.roll(kk, quarter, 1))
            k_scr[rows, :] = (kk * cos_ref[rows, :] + partner * sin_ref[rows, :]).astype(BF16)
            return carry
        lax.fori_loop(0, nk, fill, 0)

    for g in range(GQA_GROUP):
        qt = q_ref[g].astype(F32).T
        partner = jnp.concatenate([qt[quarter:2 * quarter], qt[0:quarter],
                                   qt[3 * quarter:], qt[2 * quarter:3 * quarter]], axis=0)
        qt_scr[g] = (qt * cost_ref[...] + partner * sint_ref[...]).astype(BF16)
    m_scr[...] = jnp.full(m_scr.shape, -jnp.inf, F32)
    acc_scr[...] = jnp.zeros(acc_scr.shape, F32)

    def step(c_cur, buf_cur, c_next=None, buf_next=None):
        vt = vt_scr[c_cur]
        if c_next is not None:
            start = pl.multiple_of(c_next * GQA_TK, GQA_TK)
            k = k_scr[pl.ds(start, GQA_TK), :]
        for g in range(GQA_GROUP):
            if c_next is not None:
                buf_next[g] = jnp.dot(k, qt_scr[g], preferred_element_type=F32)
            s = buf_cur[g]
            m_old = m_scr[g]
            m_new = jnp.maximum(m_old, jnp.max(s, axis=0, keepdims=True))
            alpha = jnp.exp2(m_old - m_new)
            p = jnp.exp2(s - m_new).astype(BF16)
            acc_scr[g] = alpha * acc_scr[g] + jnp.dot(vt, p, preferred_element_type=F32)
            m_scr[g] = m_new

    scores(0, sa_scr)

    def pair(j, carry):
        step(2 * j, sa_scr, 2 * j + 1, sb_scr)
        step(2 * j + 1, sb_scr, 2 * j + 2, sa_scr)
        return carry

    lax.fori_loop(0, nk // 2 - 1, pair, 0)
    step(nk - 2, sa_scr, nk - 1, sb_scr)
    step(nk - 1, sb_scr)
    for g in range(GQA_GROUP):
        cols = slice(g * HEAD_DIM, (g + 1) * HEAD_DIM)
        acc = acc_scr[g]
        o = (acc[0:HEAD_DIM] / acc[HEAD_DIM:HEAD_DIM + 1]).T
        o_ref[:, cols] = (o * z_ref[g].astype(F32)).astype(o_ref.dtype)


def _gqa(proj, batch, seq_len, cos_t, sin_t, cos_tt, sin_tt):
    t = proj.shape[1]
    assert seq_len % GQA_TQ == 0 and seq_len % (2 * GQA_TK) == 0
    nq = seq_len // GQA_TQ
    gw = GQA_GROUP * HEAD_DIM
    return pl.pallas_call(
        functools.partial(_gqa_kernel, nk=seq_len // GQA_TK),
        name="gqa",
        grid=(batch, GQA_KV_HEADS, nq),
        in_specs=[
            pl.BlockSpec((GQA_GROUP, GQA_TQ, HEAD_DIM), lambda b, g, i: (_slab(OFF_GQ_Q) // gw + g, b * nq + i, 0)),
            pl.BlockSpec((None, seq_len, HEAD_DIM), lambda b, g, i: (_slab(OFF_GQ_K) // HEAD_DIM + g, b, 0)),
            pl.BlockSpec((None, seq_len, HEAD_DIM), lambda b, g, i: (_slab(OFF_GQ_V) // HEAD_DIM + g, b, 0)),
            pl.BlockSpec((GQA_GROUP, GQA_TQ, HEAD_DIM), lambda b, g, i: (_slab(OFF_GQ_Z) // gw + g, b * nq + i, 0)),
            pl.BlockSpec((seq_len, HEAD_DIM), lambda b, g, i: (0, 0), pipeline_mode=pl.Buffered(1)),
            pl.BlockSpec((seq_len, HEAD_DIM), lambda b, g, i: (0, 0), pipeline_mode=pl.Buffered(1)),
            pl.BlockSpec((HEAD_DIM, GQA_TQ), lambda b, g, i: (0, i)),
            pl.BlockSpec((HEAD_DIM, GQA_TQ), lambda b, g, i: (0, i)),
        ],
        out_specs=pl.BlockSpec((GQA_TQ, gw), lambda b, g, i: (b * nq + i, g)),
        out_shape=jax.ShapeDtypeStruct((t, GQ_W), BF16),
        scratch_shapes=[
            pltpu.VMEM((GQA_GROUP, HEAD_DIM, GQA_TQ), BF16),
            pltpu.VMEM((seq_len, HEAD_DIM), BF16),
            pltpu.VMEM((seq_len // GQA_TK, HEAD_DIM + GQA_ONES_ROWS, GQA_TK), BF16),
            pltpu.VMEM((GQA_GROUP, 1, GQA_TQ), F32),
            pltpu.VMEM((GQA_GROUP, HEAD_DIM + GQA_ONES_ROWS, GQA_TQ), F32),
            pltpu.VMEM((GQA_GROUP, GQA_TK, GQA_TQ), F32),
            pltpu.VMEM((GQA_GROUP, GQA_TK, GQA_TQ), F32),
        ],
        compiler_params=pltpu.CompilerParams(
            dimension_semantics=("parallel", "parallel", "arbitrary"),
            vmem_limit_bytes=VMEM_LIMIT_BYTES),
    )(proj, proj, proj, proj, cos_t, sin_t, cos_tt, sin_tt)


def _tail_kernel(x_ref, oa_ref, ob_ref, ga_ref, gb_ref, wa_ref, wb_ref, wo_ref, y_ref):
    oa = jnp.concatenate([oa_ref[h] for h in range(NA_HEADS)], axis=1)
    pa = jnp.dot(oa, wa_ref[...], preferred_element_type=F32)
    pb = jnp.dot(ob_ref[...], wb_ref[...], preferred_element_type=F32)
    merged = jnp.concatenate(
        [(ga_ref[c].astype(F32) * pa[:, c * HEAD_DIM:(c + 1) * HEAD_DIM]
          + gb_ref[c].astype(F32) * pb[:, c * HEAD_DIM:(c + 1) * HEAD_DIM]).astype(BF16)
         for c in range(D_MODEL // HEAD_DIM)], axis=1)
    y_ref[...] = x_ref[...] + jnp.dot(merged, wo_ref[...], preferred_element_type=F32)


def _tail(x2d, oa, ob, proj, wa_bf, wb_bf, wo_bf):
    t = x2d.shape[0]
    assert t % TAIL_TM == 0
    resident = dict(pipeline_mode=pl.Buffered(1))
    return pl.pallas_call(
        _tail_kernel,
        name="tail",
        grid=(t // TAIL_TM,),
        in_specs=[
            pl.BlockSpec((TAIL_TM, D_MODEL), lambda i: (i, 0)),
            pl.BlockSpec((NA_HEADS, TAIL_TM, HEAD_DIM), lambda i: (0, i, 0)),
            pl.BlockSpec((TAIL_TM, GQ_W), lambda i: (i, 0)),
            pl.BlockSpec((D_MODEL // HEAD_DIM, TAIL_TM, HEAD_DIM), lambda i: (_slab(OFF_GA) // D_MODEL, i, 0)),
            pl.BlockSpec((D_MODEL // HEAD_DIM, TAIL_TM, HEAD_DIM), lambda i: (_slab(OFF_GB) // D_MODEL, i, 0)),
            pl.BlockSpec((NA_W, D_MODEL), lambda i: (0, 0), **resident),
            pl.BlockSpec((GQ_W, D_MODEL), lambda i: (0, 0), **resident),
            pl.BlockSpec((D_MODEL, D_MODEL), lambda i: (0, 0), **resident),
        ],
        out_specs=pl.BlockSpec((TAIL_TM, D_MODEL), lambda i: (i, 0)),
        out_shape=jax.ShapeDtypeStruct((t, D_MODEL), F32),
        compiler_params=pltpu.CompilerParams(
            dimension_semantics=("parallel",),
            vmem_limit_bytes=VMEM_LIMIT_BYTES),
    )(x2d, oa, ob, proj, proj, wa_bf, wb_bf, wo_bf)


def _rope_tables(seq_len):
    pos = np.arange(seq_len)
    axis = HEAD_DIM // 2
    freqs = ROPE_THETA ** (-np.arange(0, axis, 2, dtype=np.float64) / axis)
    ang_r = (pos // GRID_W)[:, None] * freqs[None, :]
    ang_c = (pos % GRID_W)[:, None] * freqs[None, :]
    cos_t = np.concatenate([np.cos(ang_r), np.cos(ang_r), np.cos(ang_c), np.cos(ang_c)], axis=-1)
    sin_t = np.concatenate([-np.sin(ang_r), np.sin(ang_r), -np.sin(ang_c), np.sin(ang_c)], axis=-1)
    tables = (cos_t, sin_t, cos_t.T, sin_t.T)
    return tuple(jnp.asarray(np.ascontiguousarray(tbl), dtype=F32) for tbl in tables)


def _layer(x, norm_g, w_in, head_g, bias_tbl, wa_bf, wb_bf, gate_bias, wo_bf):
    batch, seq_len, _ = x.shape
    x2d = x.reshape(batch * seq_len, D_MODEL)
    proj = _proj(x2d, norm_g, w_in, head_g, gate_bias)
    oa = _na(proj, batch, seq_len, bias_tbl)
    ob = _gqa(proj, batch, seq_len, *_rope_tables(seq_len))
    y = _tail(x2d, oa, ob, proj, wa_bf, wb_bf, wo_bf)
    return y.reshape(batch, seq_len, D_MODEL)


def kernel(x_prompt, x_sample, norm_g, w_in, na_q_g, na_k_g, na_rpb, gq_q_g, gq_k_g,
           w_branch_a, w_branch_b, gate_bias, w_out):
    depth = norm_g.shape[0]
    y_prompt, y_sample = x_prompt, x_sample
    for l in range(depth):
        head_g = jnp.stack([na_q_g[l] * EXP2_SCALE, na_k_g[l], gq_q_g[l] * EXP2_SCALE, gq_k_g[l]]).astype(F32)
        params = (norm_g[l][None, :].astype(F32), w_in[l].astype(BF16), head_g,
                  _na_bias_table(na_rpb[l]), w_branch_a[l].astype(BF16), w_branch_b[l].astype(BF16),
                  gate_bias[l][None, :].astype(F32), w_out[l].astype(BF16))
        y_prompt = _layer(y_prompt, *params)
        y_sample = _layer(y_sample, *params)
    return (y_prompt, y_sample)
```

```python
import functools

import numpy as np
import jax
import jax.numpy as jnp
from jax import lax
from jax.experimental import pallas as pl
from jax.experimental.pallas import tpu as pltpu

F32 = jnp.float32
BF16 = jnp.bfloat16

D_MODEL = 2048
HEAD_DIM = 128
GRID_W = 64
NA_HEADS = 8
NA_KH = 8
NA_KW = 16
GQA_HEADS = 8
GQA_KV_HEADS = 2
GQA_GROUP = GQA_HEADS // GQA_KV_HEADS
ROPE_THETA = 10000.0
EPS = 1e-6
SCALE = HEAD_DIM ** -0.5

NA_W = NA_HEADS * HEAD_DIM
GQ_W = GQA_HEADS * HEAD_DIM
KV_W = GQA_KV_HEADS * HEAD_DIM
OFF_NA_Q = 0
OFF_NA_K = OFF_NA_Q + NA_W
OFF_NA_V = OFF_NA_K + NA_W
OFF_NA_Z = OFF_NA_V + NA_W
OFF_GQ_Q = OFF_NA_Z + NA_W
OFF_GQ_K = OFF_GQ_Q + GQ_W
OFF_GQ_V = OFF_GQ_K + KV_W
OFF_GQ_Z = OFF_GQ_V + KV_W
OFF_GA = OFF_GQ_Z + GQ_W
OFF_GB = OFF_GA + D_MODEL
IN_WIDTH = OFF_GB + D_MODEL


def _slab(off):
    return (off + 2 * D_MODEL) % IN_WIDTH

V7X_VMEM_BYTES = 64 * 1024 * 1024
VMEM_LIMIT_BYTES = V7X_VMEM_BYTES * 7 // 8
BF16_SUBLANE_TILE = 16

PROJ_TM = 1024
PROJ_TN = 1536
PROJ_PAIR = 256
PROJ_WBLK = 512
NA_QROWS = 4
NA_TQ = NA_QROWS * GRID_W
NA_WCHUNKS = 3
NA_WROWS = NA_WCHUNKS * NA_QROWS
NA_TK = NA_WROWS * GRID_W
NA_NB_MAX = 16
NA_ONES_ROWS = BF16_SUBLANE_TILE
GQA_TQ = 512
GQA_TK = 512
GQA_ONES_ROWS = BF16_SUBLANE_TILE
EXP2_SCALE = SCALE * float(np.log2(np.e))
TAIL_TM = 256
NEG_BIG = -1e30


def _sigmoid(x):
    return 0.5 * jnp.tanh(0.5 * x) + 0.5


def _proj_kernel(x_ref, ng_ref, *rest):
    n_w = PROJ_TN // PROJ_WBLK
    w_refs, (hg_ref, gb_ref, o_ref, h_scr) = rest[:n_w], rest[n_w:]
    j = pl.program_id(1)

    @pl.when(j == 0)
    def _():
        x = x_ref[...]
        ms = jnp.mean(x * x, axis=-1, keepdims=True)
        h_scr[...] = (x * lax.rsqrt(ms + EPS) * ng_ref[...]).astype(BF16)

    def epilogue(col, a):
        kind, arg = _slab_head_kind(col)
        if kind == "gate":
            return _sigmoid(a + gb_ref[:, arg:arg + HEAD_DIM])
        if kind == "norm":
            ms = jnp.mean(a * a, axis=-1, keepdims=True)
            return a * lax.rsqrt(ms + EPS) * hg_ref[arg:arg + 1, :]
        if kind == "silu":
            return a * _sigmoid(a)
        return a

    for jj in range(IN_WIDTH // PROJ_TN):
        @pl.when(j == jj)
        def _(jj=jj):
            n_pairs = PROJ_TN // PROJ_PAIR
            tm = h_scr.shape[0]
            for pp in range(n_pairs):
                c0 = pp * PROJ_PAIR
                w_pair = w_refs[c0 // PROJ_WBLK][:, c0 % PROJ_WBLK:c0 % PROJ_WBLK + PROJ_PAIR]
                n_row_chunks = 2 if pp == n_pairs - 1 else 1
                for rc in range(n_row_chunks):
                    rows = slice(rc * tm // n_row_chunks, (rc + 1) * tm // n_row_chunks)
                    acc = jnp.dot(h_scr[rows, :], w_pair, preferred_element_type=F32)
                    for half in range(PROJ_PAIR // HEAD_DIM):
                        c = pp * PROJ_PAIR + half * HEAD_DIM
                        val = epilogue(jj * PROJ_TN + c, acc[:, half * HEAD_DIM:(half + 1) * HEAD_DIM])
                        o_ref[c // HEAD_DIM, rows, :] = val.astype(o_ref.dtype)


def _slab_head_kind(col):
    groups = ((OFF_GA, 2 * D_MODEL, "gate", None), (OFF_NA_Q, NA_W, "norm", 0), (OFF_NA_K, NA_W, "norm", 1),
              (OFF_NA_V, NA_W, "copy", None), (OFF_NA_Z, NA_W, "silu", None), (OFF_GQ_Q, GQ_W, "norm", 2),
              (OFF_GQ_K, KV_W, "norm", 3), (OFF_GQ_V, KV_W, "copy", None), (OFF_GQ_Z, GQ_W, "silu", None))
    for off, width, kind, arg in groups:
        if _slab(off) <= col < _slab(off) + width:
            return kind, (col - _slab(off)) if kind == "gate" else arg
    raise ValueError(col)


def _proj(x2d, norm_g, w_in, head_g, gate_bias):
    t = x2d.shape[0]
    tm = PROJ_TM
    assert t % tm == 0 and IN_WIDTH % PROJ_TN == 0 and PROJ_TN % PROJ_WBLK == 0
    assert PROJ_WBLK % PROJ_PAIR == 0 and OFF_GA % PROJ_WBLK == 0
    n_w = PROJ_TN // PROJ_WBLK
    n_wblk = IN_WIDTH // PROJ_WBLK

    def w_spec(u):
        return pl.BlockSpec((D_MODEL, PROJ_WBLK),
                            lambda i, j: (0, (n_w * j + u + OFF_GA // PROJ_WBLK) % n_wblk))

    return pl.pallas_call(
        _proj_kernel,
        name="proj",
        grid=(t // tm, IN_WIDTH // PROJ_TN),
        in_specs=[
            pl.BlockSpec((tm, D_MODEL), lambda i, j: (i, 0)),
            pl.BlockSpec((1, D_MODEL), lambda i, j: (0, 0)),
            *[w_spec(u) for u in range(n_w)],
            pl.BlockSpec((4, HEAD_DIM), lambda i, j: (0, 0)),
            pl.BlockSpec((1, 2 * D_MODEL), lambda i, j: (0, 0)),
        ],
        out_specs=pl.BlockSpec((PROJ_TN // HEAD_DIM, tm, HEAD_DIM), lambda i, j: (j, i, 0)),
        out_shape=jax.ShapeDtypeStruct((IN_WIDTH // HEAD_DIM, t, HEAD_DIM), BF16),
        scratch_shapes=[pltpu.VMEM((tm, D_MODEL), BF16)],
        compiler_params=pltpu.CompilerParams(
            dimension_semantics=("parallel", "arbitrary"),
            vmem_limit_bytes=VMEM_LIMIT_BYTES),
    )(x2d, norm_g, *([w_in] * n_w), head_g, gate_bias)


def _na_kernel(q_ref, k_ref, v_ref, z_ref, bias_ref, o_ref, vt_scr, s_scr, *, nblk):
    i = pl.program_id(2)
    nb_step = s_scr.shape[0]

    def transpose_values(c):
        rows = pl.ds(pl.multiple_of(c * NA_TQ, NA_TQ), NA_TQ)
        vt_scr[c, 0:HEAD_DIM, :] = v_ref[rows, :].astype(F32).T.astype(BF16)
        vt_scr[c, HEAD_DIM:, :] = jnp.ones((NA_ONES_ROWS, NA_TQ), BF16)

    @pl.when(i == 0)
    def _():
        transpose_values(0)

    for jb in range(nb_step):
        transpose_values(jnp.minimum(i * nb_step + jb + 1, nblk - 1))

    def first_chunk(nb):
        return jnp.clip(nb - 1, 0, nblk - NA_WCHUNKS)

    def block_scores(jb):
        start = pl.multiple_of(first_chunk(i * nb_step + jb) * NA_TQ, NA_TQ)
        k = k_ref[pl.ds(start, NA_TK), :]
        q = q_ref[jb * NA_TQ:(jb + 1) * NA_TQ, :]
        s_scr[jb] = lax.dot_general(k, q, (((1,), (1,)), ((), ())), preferred_element_type=F32)

    for jb in range(nb_step):
        block_scores(jb)

    for jb in range(nb_step):
        nb = i * nb_step + jb
        chunk0 = first_chunk(nb)
        kind = jnp.where(nb == 0, 0, jnp.where(nb == nblk - 1, 2, 1))
        logits = s_scr[jb] + bias_ref[kind, 0]
        m = jnp.max(logits, axis=0, keepdims=True)
        p = jnp.exp2(logits - m).astype(BF16)
        pv = jnp.dot(vt_scr[chunk0], p[0:NA_TQ], preferred_element_type=F32)
        for w in range(1, NA_WCHUNKS):
            pv += jnp.dot(vt_scr[chunk0 + w], p[w * NA_TQ:(w + 1) * NA_TQ],
                          preferred_element_type=F32)
        o = (pv[0:HEAD_DIM] / pv[HEAD_DIM:HEAD_DIM + 1]).T
        rows = slice(jb * NA_TQ, (jb + 1) * NA_TQ)
        o_ref[rows, :] = (o * z_ref[rows, :].astype(F32)).astype(o_ref.dtype)


def _na_bias_table(rpb):
    c = np.arange(GRID_W)[:, None]
    kc = np.arange(GRID_W)[None, :]
    cs = np.clip(c - NA_KW // 2, 0, GRID_W - NA_KW)
    col_ok = (kc >= cs) & (kc < cs + NA_KW)
    dc = kc - c + NA_KW - 1
    onehot = ((dc.T[None] == np.arange(2 * NA_KW - 1)[:, None, None]) & col_ok.T[None]).astype(np.float32)
    t1 = jnp.einsum('hrd,dkc->hrkc', rpb.astype(F32), onehot, precision=lax.Precision.HIGHEST)
    t1 = jnp.where(col_ok.T[None, None], t1 * float(np.log2(np.e)), NEG_BIG)
    t1 = jnp.pad(t1, ((0, 0), (NA_BIAS_PAD, NA_BIAS_PAD), (0, 0), (0, 0)))
    t2 = jnp.concatenate([t1[:, 1:], t1[:, :-1]], axis=-1)
    return pl.pallas_call(
        _na_bias_kernel,
        name="na_bias",
        grid=(NA_HEADS,),
        in_specs=[pl.BlockSpec((1,) + t2.shape[1:], lambda h: (h, 0, 0, 0))],
        out_specs=pl.BlockSpec((len(NA_BLOCK_OFF), 1, NA_TK, NA_TQ), lambda h: (0, h, 0, 0)),
        out_shape=jax.ShapeDtypeStruct((len(NA_BLOCK_OFF), NA_HEADS, NA_TK, NA_TQ), F32),
        compiler_params=pltpu.CompilerParams(dimension_semantics=("parallel",)),
    )(t2)


NA_BIAS_PAD = NA_QROWS
NA_BLOCK_OFF = (0, NA_KH // 2, NA_WROWS - NA_QROWS)


def _na_row_ok(kind, a, jw):
    first_key = (0, a, NA_WROWS - NA_KH)[kind]
    return first_key <= jw < first_key + NA_KH


def _na_bias_kernel(t2_ref, o_ref):
    lane = lax.broadcasted_iota(jnp.int32, (GRID_W, 2 * GRID_W), 1)
    for k, off in enumerate(NA_BLOCK_OFF):
        for jw in range(NA_WROWS):
            for a in range(0, NA_QROWS, 2):
                tile = t2_ref[0, jw - off - a + NA_KH - 1 + NA_BIAS_PAD - 1]
                ok_a, ok_b = _na_row_ok(k, a, jw), _na_row_ok(k, a + 1, jw)
                if ok_a and ok_b:
                    val = tile
                elif ok_a:
                    val = jnp.where(lane < GRID_W, tile, NEG_BIG)
                elif ok_b:
                    val = jnp.where(lane >= GRID_W, tile, NEG_BIG)
                else:
                    val = jnp.full(tile.shape, NEG_BIG, F32)
                o_ref[k, 0, jw * GRID_W:(jw + 1) * GRID_W, a * GRID_W:(a + 2) * GRID_W] = val


def _na(proj, batch, seq_len, bias_tbl):
    t = proj.shape[1]
    rows = seq_len // GRID_W
    nblk = rows // NA_QROWS
    nb_step = min(NA_NB_MAX, nblk)
    assert nblk % nb_step == 0 and nblk >= NA_WCHUNKS
    assert nb_step >= NA_WCHUNKS - 1
    nstep = nblk // nb_step
    tq = nb_step * NA_TQ
    cb = lambda off: _slab(off) // HEAD_DIM
    return pl.pallas_call(
        functools.partial(_na_kernel, nblk=nblk),
        name="na",
        grid=(NA_HEADS, batch, nstep),
        in_specs=[
            pl.BlockSpec((None, tq, HEAD_DIM), lambda h, b, i: (cb(OFF_NA_Q) + h, b * nstep + i, 0)),
            pl.BlockSpec((None, seq_len, HEAD_DIM), lambda h, b, i: (cb(OFF_NA_K) + h, b, 0)),
            pl.BlockSpec((None, seq_len, HEAD_DIM), lambda h, b, i: (cb(OFF_NA_V) + h, b, 0)),
            pl.BlockSpec((None, tq, HEAD_DIM), lambda h, b, i: (cb(OFF_NA_Z) + h, b * nstep + i, 0)),
            pl.BlockSpec((3, 1, NA_TK, NA_TQ), lambda h, b, i: (0, h, 0, 0)),
        ],
        out_specs=pl.BlockSpec((None, tq, HEAD_DIM), lambda h, b, i: (h, b * nstep + i, 0)),
        out_shape=jax.ShapeDtypeStruct((NA_HEADS, t, HEAD_DIM), BF16),
        scratch_shapes=[
            pltpu.VMEM((nblk, HEAD_DIM + NA_ONES_ROWS, NA_TQ), BF16),
            pltpu.VMEM((nb_step, NA_TK, NA_TQ), F32),
        ],
        compiler_params=pltpu.CompilerParams(
            dimension_semantics=("parallel", "parallel", "arbitrary"),
            vmem_limit_bytes=VMEM_LIMIT_BYTES),
    )(proj, proj, proj, proj, bias_tbl)


def _gqa_kernel(q_ref, k_ref, v_ref, z_ref, cos_ref, sin_ref, cost_ref, sint_ref, o_ref,
                qt_scr, k_scr, vt_scr, m_scr, acc_scr, sa_scr, sb_scr, *, nk):
    i = pl.program_id(2)
    quarter = HEAD_DIM // 4

    def scores(c, s_buf):
        start = pl.multiple_of(c * GQA_TK, GQA_TK)
        k = k_scr[pl.ds(start, GQA_TK), :]
        for g in range(GQA_GROUP):
            s_buf[g] = jnp.dot(k, qt_scr[g], preferred_element_type=F32)

    @pl.when(i == 0)
    def _():
        def fill(c, carry):
            start = pl.multiple_of(c * GQA_TK, GQA_TK)
            rows = pl.ds(start, GQA_TK)
            v = v_ref[rows, :].astype(F32)
            vt_scr[c, 0:HEAD_DIM, :] = v.T.astype(BF16)
            vt_scr[c, HEAD_DIM:, :] = jnp.ones((GQA_ONES_ROWS, GQA_TK), BF16)
            kk = k_ref[rows, :].astype(F32)
            lane = lax.broadcasted_iota(jnp.int32, kk.shape, 1)
            partner = jnp.where((lane % (2 * quarter)) < quarter,
                                pltpu.roll(kk, HEAD_DIM - quarter, 1),
                                pltpu.roll(kk, quarter, 1))
            k_scr[rows, :] = (kk * cos_ref[rows, :] + partner * sin_ref[rows, :]).astype(BF16)
            return carry
        lax.fori_loop(0, nk, fill, 0)

    for g in range(GQA_GROUP):
        qt = q_ref[g].astype(F32).T
        partner = jnp.concatenate([qt[quarter:2 * quarter], qt[0:quarter],
                                   qt[3 * quarter:], qt[2 * quarter:3 * quarter]], axis=0)
        qt_scr[g] = (qt * cost_ref[...] + partner * sint_ref[...]).astype(BF16)
    m_scr[...] = jnp.full(m_scr.shape, -jnp.inf, F32)
    acc_scr[...] = jnp.zeros(acc_scr.shape, F32)

    def step(c_cur, buf_cur, c_next=None, buf_next=None):
        vt = vt_scr[c_cur]
        if c_next is not None:
            start = pl.multiple_of(c_next * GQA_TK, GQA_TK)
            k = k_scr[pl.ds(start, GQA_TK), :]
        for g in range(GQA_GROUP):
            if c_next is not None:
                buf_next[g] = jnp.dot(k, qt_scr[g], preferred_element_type=F32)
            s = buf_cur[g]
            m_old = m_scr[g]
            m_new = jnp.maximum(m_old, jnp.max(s, axis=0, keepdims=True))
            alpha = jnp.exp2(m_old - m_new)
            p = jnp.exp2(s - m_new).astype(BF16)
            acc_scr[g] = alpha * acc_scr[g] + jnp.dot(vt, p, preferred_element_type=F32)
            m_scr[g] = m_new

    scores(0, sa_scr)

    def pair(j, carry):
        step(2 * j, sa_scr, 2 * j + 1, sb_scr)
        step(2 * j + 1, sb_scr, 2 * j + 2, sa_scr)
        return carry

    lax.fori_loop(0, nk // 2 - 1, pair, 0)
    step(nk - 2, sa_scr, nk - 1, sb_scr)
    step(nk - 1, sb_scr)
    for g in range(GQA_GROUP):
        cols = slice(g * HEAD_DIM, (g + 1) * HEAD_DIM)
        acc = acc_scr[g]
        o = (acc[0:HEAD_DIM] / acc[HEAD_DIM:HEAD_DIM + 1]).T
        o_ref[:, cols] = (o * z_ref[g].astype(F32)).astype(o_ref.dtype)


def _gqa(proj, batch, seq_len, cos_t, sin_t, cos_tt, sin_tt):
    t = proj.shape[1]
    assert seq_len % GQA_TQ == 0 and seq_len % (2 * GQA_TK) == 0
    nq = seq_len // GQA_TQ
    gw = GQA_GROUP * HEAD_DIM
    return pl.pallas_call(
        functools.partial(_gqa_kernel, nk=seq_len // GQA_TK),
        name="gqa",
        grid=(batch, GQA_KV_HEADS, nq),
        in_specs=[
            pl.BlockSpec((GQA_GROUP, GQA_TQ, HEAD_DIM), lambda b, g, i: (_slab(OFF_GQ_Q) // gw + g, b * nq + i, 0)),
            pl.BlockSpec((None, seq_len, HEAD_DIM), lambda b, g, i: (_slab(OFF_GQ_K) // HEAD_DIM + g, b, 0)),
            pl.BlockSpec((None, seq_len, HEAD_DIM), lambda b, g, i: (_slab(OFF_GQ_V) // HEAD_DIM + g, b, 0)),
            pl.BlockSpec((GQA_GROUP, GQA_TQ, HEAD_DIM), lambda b, g, i: (_slab(OFF_GQ_Z) // gw + g, b * nq + i, 0)),
            pl.BlockSpec((seq_len, HEAD_DIM), lambda b, g, i: (0, 0), pipeline_mode=pl.Buffered(1)),
            pl.BlockSpec((seq_len, HEAD_DIM), lambda b, g, i: (0, 0), pipeline_mode=pl.Buffered(1)),
            pl.BlockSpec((HEAD_DIM, GQA_TQ), lambda b, g, i: (0, i)),
            pl.BlockSpec((HEAD_DIM, GQA_TQ), lambda b, g, i: (0, i)),
        ],
        out_specs=pl.BlockSpec((GQA_TQ, gw), lambda b, g, i: (b * nq + i, g)),
        out_shape=jax.ShapeDtypeStruct((t, GQ_W), BF16),
        scratch_shapes=[
            pltpu.VMEM((GQA_GROUP, HEAD_DIM, GQA_TQ), BF16),
            pltpu.VMEM((seq_len, HEAD_DIM), BF16),
            pltpu.VMEM((seq_len // GQA_TK, HEAD_DIM + GQA_ONES_ROWS, GQA_TK), BF16),
            pltpu.VMEM((GQA_GROUP, 1, GQA_TQ), F32),
            pltpu.VMEM((GQA_GROUP, HEAD_DIM + GQA_ONES_ROWS, GQA_TQ), F32),
            pltpu.VMEM((GQA_GROUP, GQA_TK, GQA_TQ), F32),
            pltpu.VMEM((GQA_GROUP, GQA_TK, GQA_TQ), F32),
        ],
        compiler_params=pltpu.CompilerParams(
            dimension_semantics=("parallel", "parallel", "arbitrary"),
            vmem_limit_bytes=VMEM_LIMIT_BYTES),
    )(proj, proj, proj, proj, cos_t, sin_t, cos_tt, sin_tt)


def _tail_kernel(x_ref, oa_ref, ob_ref, ga_ref, gb_ref, wa_ref, wb_ref, wo_ref, y_ref):
    oa = jnp.concatenate([oa_ref[h] for h in range(NA_HEADS)], axis=1)
    pa = jnp.dot(oa, wa_ref[...], preferred_element_type=F32)
    pb = jnp.dot(ob_ref[...], wb_ref[...], preferred_element_type=F32)
    merged = jnp.concatenate(
        [(ga_ref[c].astype(F32) * pa[:, c * HEAD_DIM:(c + 1) * HEAD_DIM]
          + gb_ref[c].astype(F32) * pb[:, c * HEAD_DIM:(c + 1) * HEAD_DIM]).astype(BF16)
         for c in range(D_MODEL // HEAD_DIM)], axis=1)
    y_ref[...] = x_ref[...] + jnp.dot(merged, wo_ref[...], preferred_element_type=F32)


def _tail(x2d, oa, ob, proj, wa_bf, wb_bf, wo_bf):
    t = x2d.shape[0]
    assert t % TAIL_TM == 0
    resident = dict(pipeline_mode=pl.Buffered(1))
    return pl.pallas_call(
        _tail_kernel,
        name="tail",
        grid=(t // TAIL_TM,),
        in_specs=[
            pl.BlockSpec((TAIL_TM, D_MODEL), lambda i: (i, 0)),
            pl.BlockSpec((NA_HEADS, TAIL_TM, HEAD_DIM), lambda i: (0, i, 0)),
            pl.BlockSpec((TAIL_TM, GQ_W), lambda i: (i, 0)),
            pl.BlockSpec((D_MODEL // HEAD_DIM, TAIL_TM, HEAD_DIM), lambda i: (_slab(OFF_GA) // D_MODEL, i, 0)),
            pl.BlockSpec((D_MODEL // HEAD_DIM, TAIL_TM, HEAD_DIM), lambda i: (_slab(OFF_GB) // D_MODEL, i, 0)),
            pl.BlockSpec((NA_W, D_MODEL), lambda i: (0, 0), **resident),
            pl.BlockSpec((GQ_W, D_MODEL), lambda i: (0, 0), **resident),
            pl.BlockSpec((D_MODEL, D_MODEL), lambda i: (0, 0), **resident),
        ],
        out_specs=pl.BlockSpec((TAIL_TM, D_MODEL), lambda i: (i, 0)),
        out_shape=jax.ShapeDtypeStruct((t, D_MODEL), F32),
        compiler_params=pltpu.CompilerParams(
            dimension_semantics=("parallel",),
            vmem_limit_bytes=VMEM_LIMIT_BYTES),
    )(x2d, oa, ob, proj, proj, wa_bf, wb_bf, wo_bf)


def _rope_tables(seq_len):
    pos = np.arange(seq_len)
    axis = HEAD_DIM // 2
    freqs = ROPE_THETA ** (-np.arange(0, axis, 2, dtype=np.float64) / axis)
    ang_r = (pos // GRID_W)[:, None] * freqs[None, :]
    ang_c = (pos % GRID_W)[:, None] * freqs[None, :]
    cos_t = np.concatenate([np.cos(ang_r), np.cos(ang_r), np.cos(ang_c), np.cos(ang_c)], axis=-1)
    sin_t = np.concatenate([-np.sin(ang_r), np.sin(ang_r), -np.sin(ang_c), np.sin(ang_c)], axis=-1)
    tables = (cos_t, sin_t, cos_t.T, sin_t.T)
    return tuple(jnp.asarray(np.ascontiguousarray(tbl), dtype=F32) for tbl in tables)


def _layer(x, norm_g, w_in, head_g, bias_tbl, wa_bf, wb_bf, gate_bias, wo_bf):
    batch, seq_len, _ = x.shape
    x2d = x.reshape(batch * seq_len, D_MODEL)
    proj = _proj(x2d, norm_g, w_in, head_g, gate_bias)
    oa = _na(proj, batch, seq_len, bias_tbl)
    ob = _gqa(proj, batch, seq_len, *_rope_tables(seq_len))
    y = _tail(x2d, oa, ob, proj, wa_bf, wb_bf, wo_bf)
    return y.reshape(batch, seq_len, D_MODEL)


def kernel(x_prompt, x_sample, norm_g, w_in, na_q_g, na_k_g, na_rpb, gq_q_g, gq_k_g,
           w_branch_a, w_branch_b, gate_bias, w_out):
    depth = norm_g.shape[0]
    y_prompt, y_sample = x_prompt, x_sample
    for l in range(depth):
        head_g = jnp.stack([na_q_g[l] * EXP2_SCALE, na_k_g[l], gq_q_g[l] * EXP2_SCALE, gq_k_g[l]]).astype(F32)
        params = (norm_g[l][None, :].astype(F32), w_in[l].astype(BF16), head_g,
                  _na_bias_table(na_rpb[l]), w_branch_a[l].astype(BF16), w_branch_b[l].astype(BF16),
                  gate_bias[l][None, :].astype(F32), w_out[l].astype(BF16))
        y_prompt = _layer(y_prompt, *params)
        y_sample = _layer(y_sample, *params)
    return (y_prompt, y_sample)
```

```python
import functools

import numpy as np
import jax
import jax.numpy as jnp
from jax import lax
from jax.experimental import pallas as pl
from jax.experimental.pallas import tpu as pltpu

F32 = jnp.float32
BF16 = jnp.bfloat16

D_MODEL = 2048
HEAD_DIM = 128
GRID_W = 64
NA_HEADS = 8
NA_KH = 8
NA_KW = 16
GQA_HEADS = 8
GQA_KV_HEADS = 2
GQA_GROUP = GQA_HEADS // GQA_KV_HEADS
ROPE_THETA = 10000.0
EPS = 1e-6
SCALE = HEAD_DIM ** -0.5

NA_W = NA_HEADS * HEAD_DIM
GQ_W = GQA_HEADS * HEAD_DIM
KV_W = GQA_KV_HEADS * HEAD_DIM
OFF_NA_Q = 0
OFF_NA_K = OFF_NA_Q + NA_W
OFF_NA_V = OFF_NA_K + NA_W
OFF_NA_Z = OFF_NA_V + NA_W
OFF_GQ_Q = OFF_NA_Z + NA_W
OFF_GQ_K = OFF_GQ_Q + GQ_W
OFF_GQ_V = OFF_GQ_K + KV_W
OFF_GQ_Z = OFF_GQ_V + KV_W
OFF_GA = OFF_GQ_Z + GQ_W
OFF_GB = OFF_GA + D_MODEL
IN_WIDTH = OFF_GB + D_MODEL


def _slab(off):
    return (off + 2 * D_MODEL) % IN_WIDTH

V7X_VMEM_BYTES = 64 * 1024 * 1024
VMEM_LIMIT_BYTES = V7X_VMEM_BYTES * 7 // 8
BF16_SUBLANE_TILE = 16

PROJ_TM = 1024
PROJ_TN = 1536
PROJ_PAIR = 256
PROJ_WBLK = 512
NA_QROWS = 4
NA_TQ = NA_QROWS * GRID_W
NA_WCHUNKS = 3
NA_WROWS = NA_WCHUNKS * NA_QROWS
NA_TK = NA_WROWS * GRID_W
NA_NB_MAX = 16
NA_ONES_ROWS = BF16_SUBLANE_TILE
GQA_TQ = 512
GQA_TK = 512
GQA_ONES_ROWS = BF16_SUBLANE_TILE
EXP2_SCALE = SCALE * float(np.log2(np.e))
TAIL_TM = 512
NEG_BIG = -1e30


def _sigmoid(x):
    return 0.5 * jnp.tanh(0.5 * x) + 0.5


def _proj_kernel(x_ref, ng_ref, *rest):
    n_w = PROJ_TN // PROJ_WBLK
    w_refs, (hg_ref, gb_ref, o_ref, h_scr) = rest[:n_w], rest[n_w:]
    j = pl.program_id(1)

    @pl.when(j == 0)
    def _():
        x = x_ref[...]
        ms = jnp.mean(x * x, axis=-1, keepdims=True)
        h_scr[...] = (x * lax.rsqrt(ms + EPS) * ng_ref[...]).astype(BF16)

    def epilogue(col, a):
        kind, arg = _slab_head_kind(col)
        if kind == "gate":
            return _sigmoid(a + gb_ref[:, arg:arg + HEAD_DIM])
        if kind == "norm":
            ms = jnp.mean(a * a, axis=-1, keepdims=True)
            return a * lax.rsqrt(ms + EPS) * hg_ref[arg:arg + 1, :]
        if kind == "silu":
            return a * _sigmoid(a)
        return a

    for jj in range(IN_WIDTH // PROJ_TN):
        @pl.when(j == jj)
        def _(jj=jj):
            n_pairs = PROJ_TN // PROJ_PAIR
            tm = h_scr.shape[0]
            for pp in range(n_pairs):
                c0 = pp * PROJ_PAIR
                w_pair = w_refs[c0 // PROJ_WBLK][:, c0 % PROJ_WBLK:c0 % PROJ_WBLK + PROJ_PAIR]
                n_row_chunks = 2 if pp == n_pairs - 1 else 1
                for rc in range(n_row_chunks):
                    rows = slice(rc * tm // n_row_chunks, (rc + 1) * tm // n_row_chunks)
                    acc = jnp.dot(h_scr[rows, :], w_pair, preferred_element_type=F32)
                    for half in range(PROJ_PAIR // HEAD_DIM):
                        c = pp * PROJ_PAIR + half * HEAD_DIM
                        val = epilogue(jj * PROJ_TN + c, acc[:, half * HEAD_DIM:(half + 1) * HEAD_DIM])
                        o_ref[c // HEAD_DIM, rows, :] = val.astype(o_ref.dtype)


def _slab_head_kind(col):
    groups = ((OFF_GA, 2 * D_MODEL, "gate", None), (OFF_NA_Q, NA_W, "norm", 0), (OFF_NA_K, NA_W, "norm", 1),
              (OFF_NA_V, NA_W, "copy", None), (OFF_NA_Z, NA_W, "silu", None), (OFF_GQ_Q, GQ_W, "norm", 2),
              (OFF_GQ_K, KV_W, "norm", 3), (OFF_GQ_V, KV_W, "copy", None), (OFF_GQ_Z, GQ_W, "silu", None))
    for off, width, kind, arg in groups:
        if _slab(off) <= col < _slab(off) + width:
            return kind, (col - _slab(off)) if kind == "gate" else arg
    raise ValueError(col)


def _proj(x2d, norm_g, w_in, head_g, gate_bias):
    t = x2d.shape[0]
    tm = PROJ_TM
    assert t % tm == 0 and IN_WIDTH % PROJ_TN == 0 and PROJ_TN % PROJ_WBLK == 0
    assert PROJ_WBLK % PROJ_PAIR == 0 and OFF_GA % PROJ_WBLK == 0
    n_w = PROJ_TN // PROJ_WBLK
    n_wblk = IN_WIDTH // PROJ_WBLK

    def w_spec(u):
        return pl.BlockSpec((D_MODEL, PROJ_WBLK),
                            lambda i, j: (0, (n_w * j + u + OFF_GA // PROJ_WBLK) % n_wblk))

    return pl.pallas_call(
        _proj_kernel,
        name="proj",
        grid=(t // tm, IN_WIDTH // PROJ_TN),
        in_specs=[
            pl.BlockSpec((tm, D_MODEL), lambda i, j: (i, 0)),
            pl.BlockSpec((1, D_MODEL), lambda i, j: (0, 0)),
            *[w_spec(u) for u in range(n_w)],
            pl.BlockSpec((4, HEAD_DIM), lambda i, j: (0, 0)),
            pl.BlockSpec((1, 2 * D_MODEL), lambda i, j: (0, 0)),
        ],
        out_specs=pl.BlockSpec((PROJ_TN // HEAD_DIM, tm, HEAD_DIM), lambda i, j: (j, i, 0)),
        out_shape=jax.ShapeDtypeStruct((IN_WIDTH // HEAD_DIM, t, HEAD_DIM), BF16),
        scratch_shapes=[pltpu.VMEM((tm, D_MODEL), BF16)],
        compiler_params=pltpu.CompilerParams(
            dimension_semantics=("parallel", "arbitrary"),
            vmem_limit_bytes=VMEM_LIMIT_BYTES),
    )(x2d, norm_g, *([w_in] * n_w), head_g, gate_bias)


def _na_kernel(q_ref, k_ref, v_ref, z_ref, bias_ref, o_ref, vt_scr, s_scr, *, nblk):
    i = pl.program_id(2)
    nb_step = s_scr.shape[0]

    def transpose_values(c):
        rows = pl.ds(pl.multiple_of(c * NA_TQ, NA_TQ), NA_TQ)
        vt_scr[c, 0:HEAD_DIM, :] = v_ref[rows, :].astype(F32).T.astype(BF16)
        vt_scr[c, HEAD_DIM:, :] = jnp.ones((NA_ONES_ROWS, NA_TQ), BF16)

    @pl.when(i == 0)
    def _():
        transpose_values(0)

    for jb in range(nb_step):
        transpose_values(jnp.minimum(i * nb_step + jb + 1, nblk - 1))

    def first_chunk(nb):
        return jnp.clip(nb - 1, 0, nblk - NA_WCHUNKS)

    def block_scores(jb):
        start = pl.multiple_of(first_chunk(i * nb_step + jb) * NA_TQ, NA_TQ)
        k = k_ref[pl.ds(start, NA_TK), :]
        q = q_ref[jb * NA_TQ:(jb + 1) * NA_TQ, :]
        s_scr[jb] = lax.dot_general(k, q, (((1,), (1,)), ((), ())), preferred_element_type=F32)

    for jb in range(nb_step):
        block_scores(jb)

    for jb in range(nb_step):
        nb = i * nb_step + jb
        chunk0 = first_chunk(nb)
        kind = jnp.where(nb == 0, 0, jnp.where(nb == nblk - 1, 2, 1))
        logits = s_scr[jb] + bias_ref[kind, 0]
        m = jnp.max(logits, axis=0, keepdims=True)
        p = jnp.exp2(logits - m).astype(BF16)
        pv = jnp.dot(vt_scr[chunk0], p[0:NA_TQ], preferred_element_type=F32)
        for w in range(1, NA_WCHUNKS):
            pv += jnp.dot(vt_scr[chunk0 + w], p[w * NA_TQ:(w + 1) * NA_TQ],
                          preferred_element_type=F32)
        o = (pv[0:HEAD_DIM] / pv[HEAD_DIM:HEAD_DIM + 1]).T
        rows = slice(jb * NA_TQ, (jb + 1) * NA_TQ)
        o_ref[rows, :] = (o * z_ref[rows, :].astype(F32)).astype(o_ref.dtype)


def _na_bias_table(rpb):
    c = np.arange(GRID_W)[:, None]
    kc = np.arange(GRID_W)[None, :]
    cs = np.clip(c - NA_KW // 2, 0, GRID_W - NA_KW)
    col_ok = (kc >= cs) & (kc < cs + NA_KW)
    dc = kc - c + NA_KW - 1
    onehot = ((dc.T[None] == np.arange(2 * NA_KW - 1)[:, None, None]) & col_ok.T[None]).astype(np.float32)
    t1 = jnp.einsum('hrd,dkc->hrkc', rpb.astype(F32), onehot, precision=lax.Precision.HIGHEST)
    t1 = jnp.where(col_ok.T[None, None], t1 * float(np.log2(np.e)), NEG_BIG)
    t1 = jnp.pad(t1, ((0, 0), (NA_BIAS_PAD, NA_BIAS_PAD), (0, 0), (0, 0)))
    t2 = jnp.concatenate([t1[:, 1:], t1[:, :-1]], axis=-1)
    return pl.pallas_call(
        _na_bias_kernel,
        name="na_bias",
        grid=(NA_HEADS,),
        in_specs=[pl.BlockSpec((1,) + t2.shape[1:], lambda h: (h, 0, 0, 0))],
        out_specs=pl.BlockSpec((len(NA_BLOCK_OFF), 1, NA_TK, NA_TQ), lambda h: (0, h, 0, 0)),
        out_shape=jax.ShapeDtypeStruct((len(NA_BLOCK_OFF), NA_HEADS, NA_TK, NA_TQ), F32),
        compiler_params=pltpu.CompilerParams(dimension_semantics=("parallel",)),
    )(t2)


NA_BIAS_PAD = NA_QROWS
NA_BLOCK_OFF = (0, NA_KH // 2, NA_WROWS - NA_QROWS)


def _na_row_ok(kind, a, jw):
    first_key = (0, a, NA_WROWS - NA_KH)[kind]
    return first_key <= jw < first_key + NA_KH


def _na_bias_kernel(t2_ref, o_ref):
    lane = lax.broadcasted_iota(jnp.int32, (GRID_W, 2 * GRID_W), 1)
    for k, off in enumerate(NA_BLOCK_OFF):
        for jw in range(NA_WROWS):
            for a in range(0, NA_QROWS, 2):
                tile = t2_ref[0, jw - off - a + NA_KH - 1 + NA_BIAS_PAD - 1]
                ok_a, ok_b = _na_row_ok(k, a, jw), _na_row_ok(k, a + 1, jw)
                if ok_a and ok_b:
                    val = tile
                elif ok_a:
                    val = jnp.where(lane < GRID_W, tile, NEG_BIG)
                elif ok_b:
                    val = jnp.where(lane >= GRID_W, tile, NEG_BIG)
                else:
                    val = jnp.full(tile.shape, NEG_BIG, F32)
                o_ref[k, 0, jw * GRID_W:(jw + 1) * GRID_W, a * GRID_W:(a + 2) * GRID_W] = val


def _na(proj, batch, seq_len, bias_tbl):
    t = proj.shape[1]
    rows = seq_len // GRID_W
    nblk = rows // NA_QROWS
    nb_step = min(NA_NB_MAX, nblk)
    assert nblk % nb_step == 0 and nblk >= NA_WCHUNKS
    assert nb_step >= NA_WCHUNKS - 1
    nstep = nblk // nb_step
    tq = nb_step * NA_TQ
    cb = lambda off: _slab(off) // HEAD_DIM
    return pl.pallas_call(
        functools.partial(_na_kernel, nblk=nblk),
        name="na",
        grid=(NA_HEADS, batch, nstep),
        in_specs=[
            pl.BlockSpec((None, tq, HEAD_DIM), lambda h, b, i: (cb(OFF_NA_Q) + h, b * nstep + i, 0)),
            pl.BlockSpec((None, seq_len, HEAD_DIM), lambda h, b, i: (cb(OFF_NA_K) + h, b, 0)),
            pl.BlockSpec((None, seq_len, HEAD_DIM), lambda h, b, i: (cb(OFF_NA_V) + h, b, 0)),
            pl.BlockSpec((None, tq, HEAD_DIM), lambda h, b, i: (cb(OFF_NA_Z) + h, b * nstep + i, 0)),
            pl.BlockSpec((3, 1, NA_TK, NA_TQ), lambda h, b, i: (0, h, 0, 0)),
        ],
        out_specs=pl.BlockSpec((None, tq, HEAD_DIM), lambda h, b, i: (h, b * nstep + i, 0)),
        out_shape=jax.ShapeDtypeStruct((NA_HEADS, t, HEAD_DIM), BF16),
        scratch_shapes=[
            pltpu.VMEM((nblk, HEAD_DIM + NA_ONES_ROWS, NA_TQ), BF16),
            pltpu.VMEM((nb_step, NA_TK, NA_TQ), F32),
        ],
        compiler_params=pltpu.CompilerParams(
            dimension_semantics=("parallel", "parallel", "arbitrary"),
            vmem_limit_bytes=VMEM_LIMIT_BYTES),
    )(proj, proj, proj, proj, bias_tbl)


def _gqa_kernel(q_ref, k_ref, v_ref, z_ref, cos_ref, sin_ref, cost_ref, sint_ref, o_ref,
                qt_scr, k_scr, vt_scr, m_scr, acc_scr, sa_scr, sb_scr, *, nk):
    i = pl.program_id(2)
    quarter = HEAD_DIM // 4

    def scores(c, s_buf):
        start = pl.multiple_of(c * GQA_TK, GQA_TK)
        k = k_scr[pl.ds(start, GQA_TK), :]
        for g in range(GQA_GROUP):
            s_buf[g] = jnp.dot(k, qt_scr[g], preferred_element_type=F32)

    @pl.when(i == 0)
    def _():
        def fill(c, carry):
            start = pl.multiple_of(c * GQA_TK, GQA_TK)
            rows = pl.ds(start, GQA_TK)
            v = v_ref[rows, :].astype(F32)
            vt_scr[c, 0:HEAD_DIM, :] = v.T.astype(BF16)
            vt_scr[c, HEAD_DIM:, :] = jnp.ones((GQA_ONES_ROWS, GQA_TK), BF16)
            kk = k_ref[rows, :].astype(F32)
            lane = lax.broadcasted_iota(jnp.int32, kk.shape, 1)
            partner = jnp.where((lane % (2 * quarter)) < quarter,
                                pltpu.roll(kk, HEAD_DIM - quarter, 1),
                                pltpu.roll(kk, quarter, 1))
            k_scr[rows, :] = (kk * cos_ref[rows, :] + partner * sin_ref[rows, :]).astype(BF16)
            return carry
        lax.fori_loop(0, nk, fill, 0)

    for g in range(GQA_GROUP):
        qt = q_ref[g].astype(F32).T
        partner = jnp.concatenate([qt[quarter:2 * quarter], qt[0:quarter],
                                   qt[3 * quarter:], qt[2 * quarter:3 * quarter]], axis=0)
        qt_scr[g] = (qt * cost_ref[...] + partner * sint_ref[...]).astype(BF16)
    m_scr[...] = jnp.full(m_scr.shape, -jnp.inf, F32)
    acc_scr[...] = jnp.zeros(acc_scr.shape, F32)

    def step(c_cur, buf_cur, c_next=None, buf_next=None):
        vt = vt_scr[c_cur]
        if c_next is not None:
            start = pl.multiple_of(c_next * GQA_TK, GQA_TK)
            k = k_scr[pl.ds(start, GQA_TK), :]
        for g in range(GQA_GROUP):
            if c_next is not None:
                buf_next[g] = jnp.dot(k, qt_scr[g], preferred_element_type=F32)
            s = buf_cur[g]
            m_old = m_scr[g]
            m_new = jnp.maximum(m_old, jnp.max(s, axis=0, keepdims=True))
            alpha = jnp.exp2(m_old - m_new)
            p = jnp.exp2(s - m_new).astype(BF16)
            acc_scr[g] = alpha * acc_scr[g] + jnp.dot(vt, p, preferred_element_type=F32)
            m_scr[g] = m_new

    scores(0, sa_scr)

    def pair(j, carry):
        step(2 * j, sa_scr, 2 * j + 1, sb_scr)
        step(2 * j + 1, sb_scr, 2 * j + 2, sa_scr)
        return carry

    lax.fori_loop(0, nk // 2 - 1, pair, 0)
    step(nk - 2, sa_scr, nk - 1, sb_scr)
    step(nk - 1, sb_scr)
    for g in range(GQA_GROUP):
        cols = slice(g * HEAD_DIM, (g + 1) * HEAD_DIM)
        acc = acc_scr[g]
        o = (acc[0:HEAD_DIM] / acc[HEAD_DIM:HEAD_DIM + 1]).T
        o_ref[:, cols] = (o * z_ref[g].astype(F32)).astype(o_ref.dtype)


def _gqa(proj, batch, seq_len, cos_t, sin_t, cos_tt, sin_tt):
    t = proj.shape[1]
    assert seq_len % GQA_TQ == 0 and seq_len % (2 * GQA_TK) == 0
    nq = seq_len // GQA_TQ
    gw = GQA_GROUP * HEAD_DIM
    return pl.pallas_call(
        functools.partial(_gqa_kernel, nk=seq_len // GQA_TK),
        name="gqa",
        grid=(batch, GQA_KV_HEADS, nq),
        in_specs=[
            pl.BlockSpec((GQA_GROUP, GQA_TQ, HEAD_DIM), lambda b, g, i: (_slab(OFF_GQ_Q) // gw + g, b * nq + i, 0)),
            pl.BlockSpec((None, seq_len, HEAD_DIM), lambda b, g, i: (_slab(OFF_GQ_K) // HEAD_DIM + g, b, 0)),
            pl.BlockSpec((None, seq_len, HEAD_DIM), lambda b, g, i: (_slab(OFF_GQ_V) // HEAD_DIM + g, b, 0)),
            pl.BlockSpec((GQA_GROUP, GQA_TQ, HEAD_DIM), lambda b, g, i: (_slab(OFF_GQ_Z) // gw + g, b * nq + i, 0)),
            pl.BlockSpec((seq_len, HEAD_DIM), lambda b, g, i: (0, 0), pipeline_mode=pl.Buffered(1)),
            pl.BlockSpec((seq_len, HEAD_DIM), lambda b, g, i: (0, 0), pipeline_mode=pl.Buffered(1)),
            pl.BlockSpec((HEAD_DIM, GQA_TQ), lambda b, g, i: (0, i)),
            pl.BlockSpec((HEAD_DIM, GQA_TQ), lambda b, g, i: (0, i)),
        ],
        out_specs=pl.BlockSpec((GQA_TQ, gw), lambda b, g, i: (b * nq + i, g)),
        out_shape=jax.ShapeDtypeStruct((t, GQ_W), BF16),
        scratch_shapes=[
            pltpu.VMEM((GQA_GROUP, HEAD_DIM, GQA_TQ), BF16),
            pltpu.VMEM((seq_len, HEAD_DIM), BF16),
            pltpu.VMEM((seq_len // GQA_TK, HEAD_DIM + GQA_ONES_ROWS, GQA_TK), BF16),
            pltpu.VMEM((GQA_GROUP, 1, GQA_TQ), F32),
            pltpu.VMEM((GQA_GROUP, HEAD_DIM + GQA_ONES_ROWS, GQA_TQ), F32),
            pltpu.VMEM((GQA_GROUP, GQA_TK, GQA_TQ), F32),
            pltpu.VMEM((GQA_GROUP, GQA_TK, GQA_TQ), F32),
        ],
        compiler_params=pltpu.CompilerParams(
            dimension_semantics=("parallel", "parallel", "arbitrary"),
            vmem_limit_bytes=VMEM_LIMIT_BYTES),
    )(proj, proj, proj, proj, cos_t, sin_t, cos_tt, sin_tt)


def _tail_kernel(x_ref, oa_ref, ob_ref, ga_ref, gb_ref, wa_ref, wb_ref, wo_ref, y_ref):
    oa = jnp.concatenate([oa_ref[h] for h in range(NA_HEADS)], axis=1)
    pa = jnp.dot(oa, wa_ref[...], preferred_element_type=F32)
    pb = jnp.dot(ob_ref[...], wb_ref[...], preferred_element_type=F32)
    merged = jnp.concatenate(
        [(ga_ref[c].astype(F32) * pa[:, c * HEAD_DIM:(c + 1) * HEAD_DIM]
          + gb_ref[c].astype(F32) * pb[:, c * HEAD_DIM:(c + 1) * HEAD_DIM]).astype(BF16)
         for c in range(D_MODEL // HEAD_DIM)], axis=1)
    y_ref[...] = x_ref[...] + jnp.dot(merged, wo_ref[...], preferred_element_type=F32)


def _tail(x2d, oa, ob, proj, wa_bf, wb_bf, wo_bf):
    t = x2d.shape[0]
    assert t % TAIL_TM == 0
    resident = dict(pipeline_mode=pl.Buffered(1))
    return pl.pallas_call(
        _tail_kernel,
        name="tail",
        grid=(t // TAIL_TM,),
        in_specs=[
            pl.BlockSpec((TAIL_TM, D_MODEL), lambda i: (i, 0)),
            pl.BlockSpec((NA_HEADS, TAIL_TM, HEAD_DIM), lambda i: (0, i, 0)),
            pl.BlockSpec((TAIL_TM, GQ_W), lambda i: (i, 0)),
            pl.BlockSpec((D_MODEL // HEAD_DIM, TAIL_TM, HEAD_DIM), lambda i: (_slab(OFF_GA) // D_MODEL, i, 0)),
            pl.BlockSpec((D_MODEL // HEAD_DIM, TAIL_TM, HEAD_DIM), lambda i: (_slab(OFF_GB) // D_MODEL, i, 0)),
            pl.BlockSpec((NA_W, D_MODEL), lambda i: (0, 0), **resident),
            pl.BlockSpec((GQ_W, D_MODEL), lambda i: (0, 0), **resident),
            pl.BlockSpec((D_MODEL, D_MODEL), lambda i: (0, 0), **resident),
        ],
        out_specs=pl.BlockSpec((TAIL_TM, D_MODEL), lambda i: (i, 0)),
        out_shape=jax.ShapeDtypeStruct((t, D_MODEL), F32),
        compiler_params=pltpu.CompilerParams(
            dimension_semantics=("parallel",),
            vmem_limit_bytes=VMEM_LIMIT_BYTES),
    )(x2d, oa, ob, proj, proj, wa_bf, wb_bf, wo_bf)


def _rope_tables(seq_len):
    pos = np.arange(seq_len)
    axis = HEAD_DIM // 2
    freqs = ROPE_THETA ** (-np.arange(0, axis, 2, dtype=np.float64) / axis)
    ang_r = (pos // GRID_W)[:, None] * freqs[None, :]
    ang_c = (pos % GRID_W)[:, None] * freqs[None, :]
    cos_t = np.concatenate([np.cos(ang_r), np.cos(ang_r), np.cos(ang_c), np.cos(ang_c)], axis=-1)
    sin_t = np.concatenate([-np.sin(ang_r), np.sin(ang_r), -np.sin(ang_c), np.sin(ang_c)], axis=-1)
    tables = (cos_t, sin_t, cos_t.T, sin_t.T)
    return tuple(jnp.asarray(np.ascontiguousarray(tbl), dtype=F32) for tbl in tables)


def _layer(x, norm_g, w_in, head_g, bias_tbl, wa_bf, wb_bf, gate_bias, wo_bf):
    batch, seq_len, _ = x.shape
    x2d = x.reshape(batch * seq_len, D_MODEL)
    proj = _proj(x2d, norm_g, w_in, head_g, gate_bias)
    oa = _na(proj, batch, seq_len, bias_tbl)
    ob = _gqa(proj, batch, seq_len, *_rope_tables(seq_len))
    y = _tail(x2d, oa, ob, proj, wa_bf, wb_bf, wo_bf)
    return y.reshape(batch, seq_len, D_MODEL)


def kernel(x_prompt, x_sample, norm_g, w_in, na_q_g, na_k_g, na_rpb, gq_q_g, gq_k_g,
           w_branch_a, w_branch_b, gate_bias, w_out):
    depth = norm_g.shape[0]
    y_prompt, y_sample = x_prompt, x_sample
    for l in range(depth):
        head_g = jnp.stack([na_q_g[l] * EXP2_SCALE, na_k_g[l], gq_q_g[l] * EXP2_SCALE, gq_k_g[l]]).astype(F32)
        params = (norm_g[l][None, :].astype(F32), w_in[l].astype(BF16), head_g,
                  _na_bias_table(na_rpb[l]), w_branch_a[l].astype(BF16), w_branch_b[l].astype(BF16),
                  gate_bias[l][None, :].astype(F32), w_out[l].astype(BF16))
        y_prompt = _layer(y_prompt, *params)
        y_sample = _layer(y_sample, *params)
    return (y_prompt, y_sample)
```

```python
import functools

import numpy as np
import jax
import jax.numpy as jnp
from jax import lax
from jax.experimental import pallas as pl
from jax.experimental.pallas import tpu as pltpu

F32 = jnp.float32
BF16 = jnp.bfloat16

D_MODEL = 2048
HEAD_DIM = 128
GRID_W = 64
NA_HEADS = 8
NA_KH = 8
NA_KW = 16
GQA_HEADS = 8
GQA_KV_HEADS = 2
GQA_GROUP = GQA_HEADS // GQA_KV_HEADS
ROPE_THETA = 10000.0
EPS = 1e-6
SCALE = HEAD_DIM ** -0.5

NA_W = NA_HEADS * HEAD_DIM
GQ_W = GQA_HEADS * HEAD_DIM
KV_W = GQA_KV_HEADS * HEAD_DIM
OFF_NA_Q = 0
OFF_NA_K = OFF_NA_Q + NA_W
OFF_NA_V = OFF_NA_K + NA_W
OFF_NA_Z = OFF_NA_V + NA_W
OFF_GQ_Q = OFF_NA_Z + NA_W
OFF_GQ_K = OFF_GQ_Q + GQ_W
OFF_GQ_V = OFF_GQ_K + KV_W
OFF_GQ_Z = OFF_GQ_V + KV_W
OFF_GA = OFF_GQ_Z + GQ_W
OFF_GB = OFF_GA + D_MODEL
IN_WIDTH = OFF_GB + D_MODEL


def _slab(off):
    return (off + 2 * D_MODEL) % IN_WIDTH

V7X_VMEM_BYTES = 64 * 1024 * 1024
VMEM_LIMIT_BYTES = V7X_VMEM_BYTES * 7 // 8
BF16_SUBLANE_TILE = 16

PROJ_TM = 1024
PROJ_TN = 1536
PROJ_PAIR = 256
PROJ_WBLK = 512
NA_QROWS = 4
NA_TQ = NA_QROWS * GRID_W
NA_WCHUNKS = 3
NA_WROWS = NA_WCHUNKS * NA_QROWS
NA_TK = NA_WROWS * GRID_W
NA_NB_MAX = 16
NA_ONES_ROWS = BF16_SUBLANE_TILE
GQA_TQ = 512
GQA_TK = 512
GQA_FEW_CHUNKS = 4
GQA_ONES_ROWS = BF16_SUBLANE_TILE
EXP2_SCALE = SCALE * float(np.log2(np.e))
TAIL_TM = 512
NEG_BIG = -1e30


def _sigmoid(x):
    return 0.5 * jnp.tanh(0.5 * x) + 0.5


def _proj_kernel(x_ref, ng_ref, *rest):
    n_w = PROJ_TN // PROJ_WBLK
    w_refs, (hg_ref, gb_ref, o_ref, h_scr) = rest[:n_w], rest[n_w:]
    j = pl.program_id(1)

    @pl.when(j == 0)
    def _():
        x = x_ref[...]
        ms = jnp.mean(x * x, axis=-1, keepdims=True)
        h_scr[...] = (x * lax.rsqrt(ms + EPS) * ng_ref[...]).astype(BF16)

    def epilogue(col, a):
        kind, arg = _slab_head_kind(col)
        if kind == "gate":
            return _sigmoid(a + gb_ref[:, arg:arg + HEAD_DIM])
        if kind == "norm":
            ms = jnp.mean(a * a, axis=-1, keepdims=True)
            return a * lax.rsqrt(ms + EPS) * hg_ref[arg:arg + 1, :]
        if kind == "silu":
            return a * _sigmoid(a)
        return a

    for jj in range(IN_WIDTH // PROJ_TN):
        @pl.when(j == jj)
        def _(jj=jj):
            n_pairs = PROJ_TN // PROJ_PAIR
            tm = h_scr.shape[0]
            for pp in range(n_pairs):
                c0 = pp * PROJ_PAIR
                w_pair = w_refs[c0 // PROJ_WBLK][:, c0 % PROJ_WBLK:c0 % PROJ_WBLK + PROJ_PAIR]
                n_row_chunks = 2 if pp == n_pairs - 1 else 1
                for rc in range(n_row_chunks):
                    rows = slice(rc * tm // n_row_chunks, (rc + 1) * tm // n_row_chunks)
                    acc = jnp.dot(h_scr[rows, :], w_pair, preferred_element_type=F32)
                    for half in range(PROJ_PAIR // HEAD_DIM):
                        c = pp * PROJ_PAIR + half * HEAD_DIM
                        val = epilogue(jj * PROJ_TN + c, acc[:, half * HEAD_DIM:(half + 1) * HEAD_DIM])
                        o_ref[c // HEAD_DIM, rows, :] = val.astype(o_ref.dtype)


def _slab_head_kind(col):
    groups = ((OFF_GA, 2 * D_MODEL, "gate", None), (OFF_NA_Q, NA_W, "norm", 0), (OFF_NA_K, NA_W, "norm", 1),
              (OFF_NA_V, NA_W, "copy", None), (OFF_NA_Z, NA_W, "silu", None), (OFF_GQ_Q, GQ_W, "norm", 2),
              (OFF_GQ_K, KV_W, "norm", 3), (OFF_GQ_V, KV_W, "copy", None), (OFF_GQ_Z, GQ_W, "silu", None))
    for off, width, kind, arg in groups:
        if _slab(off) <= col < _slab(off) + width:
            return kind, (col - _slab(off)) if kind == "gate" else arg
    raise ValueError(col)


def _proj(x2d, norm_g, w_in, head_g, gate_bias):
    t = x2d.shape[0]
    tm = PROJ_TM
    assert t % tm == 0 and IN_WIDTH % PROJ_TN == 0 and PROJ_TN % PROJ_WBLK == 0
    assert PROJ_WBLK % PROJ_PAIR == 0 and OFF_GA % PROJ_WBLK == 0
    n_w = PROJ_TN // PROJ_WBLK
    n_wblk = IN_WIDTH // PROJ_WBLK

    def w_spec(u):
        return pl.BlockSpec((D_MODEL, PROJ_WBLK),
                            lambda i, j: (0, (n_w * j + u + OFF_GA // PROJ_WBLK) % n_wblk))

    return pl.pallas_call(
        _proj_kernel,
        name="proj",
        grid=(t // tm, IN_WIDTH // PROJ_TN),
        in_specs=[
            pl.BlockSpec((tm, D_MODEL), lambda i, j: (i, 0)),
            pl.BlockSpec((1, D_MODEL), lambda i, j: (0, 0)),
            *[w_spec(u) for u in range(n_w)],
            pl.BlockSpec((4, HEAD_DIM), lambda i, j: (0, 0)),
            pl.BlockSpec((1, 2 * D_MODEL), lambda i, j: (0, 0)),
        ],
        out_specs=pl.BlockSpec((PROJ_TN // HEAD_DIM, tm, HEAD_DIM), lambda i, j: (j, i, 0)),
        out_shape=jax.ShapeDtypeStruct((IN_WIDTH // HEAD_DIM, t, HEAD_DIM), BF16),
        scratch_shapes=[pltpu.VMEM((tm, D_MODEL), BF16)],
        compiler_params=pltpu.CompilerParams(
            dimension_semantics=("parallel", "arbitrary"),
            vmem_limit_bytes=VMEM_LIMIT_BYTES),
    )(x2d, norm_g, *([w_in] * n_w), head_g, gate_bias)


def _na_kernel(q_ref, k_ref, v_ref, z_ref, bias_ref, o_ref, vt_scr, s_scr, *, nblk):
    i = pl.program_id(2)
    nb_step = s_scr.shape[0]

    def transpose_values(c):
        rows = pl.ds(pl.multiple_of(c * NA_TQ, NA_TQ), NA_TQ)
        vt_scr[c, 0:HEAD_DIM, :] = v_ref[rows, :].astype(F32).T.astype(BF16)
        vt_scr[c, HEAD_DIM:, :] = jnp.ones((NA_ONES_ROWS, NA_TQ), BF16)

    @pl.when(i == 0)
    def _():
        transpose_values(0)

    for jb in range(nb_step):
        transpose_values(jnp.minimum(i * nb_step + jb + 1, nblk - 1))

    def first_chunk(nb):
        return jnp.clip(nb - 1, 0, nblk - NA_WCHUNKS)

    def block_scores(jb):
        start = pl.multiple_of(first_chunk(i * nb_step + jb) * NA_TQ, NA_TQ)
        k = k_ref[pl.ds(start, NA_TK), :]
        q = q_ref[jb * NA_TQ:(jb + 1) * NA_TQ, :]
        s_scr[jb] = lax.dot_general(k, q, (((1,), (1,)), ((), ())), preferred_element_type=F32)

    for jb in range(nb_step):
        block_scores(jb)

    for jb in range(nb_step):
        nb = i * nb_step + jb
        chunk0 = first_chunk(nb)
        kind = jnp.where(nb == 0, 0, jnp.where(nb == nblk - 1, 2, 1))
        logits = s_scr[jb] + bias_ref[kind, 0]
        m = jnp.max(logits, axis=0, keepdims=True)
        p = jnp.exp2(logits - m).astype(BF16)
        pv = jnp.dot(vt_scr[chunk0], p[0:NA_TQ], preferred_element_type=F32)
        for w in range(1, NA_WCHUNKS):
            pv += jnp.dot(vt_scr[chunk0 + w], p[w * NA_TQ:(w + 1) * NA_TQ],
                          preferred_element_type=F32)
        o = (pv[0:HEAD_DIM] / pv[HEAD_DIM:HEAD_DIM + 1]).T
        rows = slice(jb * NA_TQ, (jb + 1) * NA_TQ)
        o_ref[rows, :] = (o * z_ref[rows, :].astype(F32)).astype(o_ref.dtype)


def _na_bias_table(rpb):
    c = np.arange(GRID_W)[:, None]
    kc = np.arange(GRID_W)[None, :]
    cs = np.clip(c - NA_KW // 2, 0, GRID_W - NA_KW)
    col_ok = (kc >= cs) & (kc < cs + NA_KW)
    dc = kc - c + NA_KW - 1
    onehot = ((dc.T[None] == np.arange(2 * NA_KW - 1)[:, None, None]) & col_ok.T[None]).astype(np.float32)
    t1 = jnp.einsum('hrd,dkc->hrkc', rpb.astype(F32), onehot, precision=lax.Precision.HIGHEST)
    t1 = jnp.where(col_ok.T[None, None], t1 * float(np.log2(np.e)), NEG_BIG)
    t1 = jnp.pad(t1, ((0, 0), (NA_BIAS_PAD, NA_BIAS_PAD), (0, 0), (0, 0)))
    t2 = jnp.concatenate([t1[:, 1:], t1[:, :-1]], axis=-1)
    return pl.pallas_call(
        _na_bias_kernel,
        name="na_bias",
        grid=(NA_HEADS,),
        in_specs=[pl.BlockSpec((1,) + t2.shape[1:], lambda h: (h, 0, 0, 0))],
        out_specs=pl.BlockSpec((len(NA_BLOCK_OFF), 1, NA_TK, NA_TQ), lambda h: (0, h, 0, 0)),
        out_shape=jax.ShapeDtypeStruct((len(NA_BLOCK_OFF), NA_HEADS, NA_TK, NA_TQ), F32),
        compiler_params=pltpu.CompilerParams(dimension_semantics=("parallel",)),
    )(t2)


NA_BIAS_PAD = NA_QROWS
NA_BLOCK_OFF = (0, NA_KH // 2, NA_WROWS - NA_QROWS)


def _na_row_ok(kind, a, jw):
    first_key = (0, a, NA_WROWS - NA_KH)[kind]
    return first_key <= jw < first_key + NA_KH


def _na_bias_kernel(t2_ref, o_ref):
    lane = lax.broadcasted_iota(jnp.int32, (GRID_W, 2 * GRID_W), 1)
    for k, off in enumerate(NA_BLOCK_OFF):
        for jw in range(NA_WROWS):
            for a in range(0, NA_QROWS, 2):
                tile = t2_ref[0, jw - off - a + NA_KH - 1 + NA_BIAS_PAD - 1]
                ok_a, ok_b = _na_row_ok(k, a, jw), _na_row_ok(k, a + 1, jw)
                if ok_a and ok_b:
                    val = tile
                elif ok_a:
                    val = jnp.where(lane < GRID_W, tile, NEG_BIG)
                elif ok_b:
                    val = jnp.where(lane >= GRID_W, tile, NEG_BIG)
                else:
                    val = jnp.full(tile.shape, NEG_BIG, F32)
                o_ref[k, 0, jw * GRID_W:(jw + 1) * GRID_W, a * GRID_W:(a + 2) * GRID_W] = val


def _na(proj, batch, seq_len, bias_tbl):
    t = proj.shape[1]
    rows = seq_len // GRID_W
    nblk = rows // NA_QROWS
    nb_step = min(NA_NB_MAX, nblk)
    assert nblk % nb_step == 0 and nblk >= NA_WCHUNKS
    assert nb_step >= NA_WCHUNKS - 1
    nstep = nblk // nb_step
    tq = nb_step * NA_TQ
    cb = lambda off: _slab(off) // HEAD_DIM
    return pl.pallas_call(
        functools.partial(_na_kernel, nblk=nblk),
        name="na",
        grid=(NA_HEADS, batch, nstep),
        in_specs=[
            pl.BlockSpec((None, tq, HEAD_DIM), lambda h, b, i: (cb(OFF_NA_Q) + h, b * nstep + i, 0)),
            pl.BlockSpec((None, seq_len, HEAD_DIM), lambda h, b, i: (cb(OFF_NA_K) + h, b, 0)),
            pl.BlockSpec((None, seq_len, HEAD_DIM), lambda h, b, i: (cb(OFF_NA_V) + h, b, 0)),
            pl.BlockSpec((None, tq, HEAD_DIM), lambda h, b, i: (cb(OFF_NA_Z) + h, b * nstep + i, 0)),
            pl.BlockSpec((3, 1, NA_TK, NA_TQ), lambda h, b, i: (0, h, 0, 0)),
        ],
        out_specs=pl.BlockSpec((None, tq, HEAD_DIM), lambda h, b, i: (h, b * nstep + i, 0)),
        out_shape=jax.ShapeDtypeStruct((NA_HEADS, t, HEAD_DIM), BF16),
        scratch_shapes=[
            pltpu.VMEM((nblk, HEAD_DIM + NA_ONES_ROWS, NA_TQ), BF16),
            pltpu.VMEM((nb_step, NA_TK, NA_TQ), F32),
        ],
        compiler_params=pltpu.CompilerParams(
            dimension_semantics=("parallel", "parallel", "arbitrary"),
            vmem_limit_bytes=VMEM_LIMIT_BYTES),
    )(proj, proj, proj, proj, bias_tbl)


def _gqa_kernel(q_ref, k_ref, v_ref, z_ref, cos_ref, sin_ref, cost_ref, sint_ref, o_ref,
                qt_scr, k_scr, vt_scr, m_scr, acc_scr, sa_scr, sb_scr, *, nk):
    i = pl.program_id(2)
    quarter = HEAD_DIM // 4

    def scores(c, s_buf):
        start = pl.multiple_of(c * GQA_TK, GQA_TK)
        k = k_scr[pl.ds(start, GQA_TK), :]
        for g in range(GQA_GROUP):
            s_buf[g] = jnp.dot(k, qt_scr[g], preferred_element_type=F32)

    @pl.when(i == 0)
    def _():
        def fill(c, carry):
            start = pl.multiple_of(c * GQA_TK, GQA_TK)
            rows = pl.ds(start, GQA_TK)
            v = v_ref[rows, :].astype(F32)
            vt_scr[c, 0:HEAD_DIM, :] = v.T.astype(BF16)
            vt_scr[c, HEAD_DIM:, :] = jnp.ones((GQA_ONES_ROWS, GQA_TK), BF16)
            kk = k_ref[rows, :].astype(F32)
            lane = lax.broadcasted_iota(jnp.int32, kk.shape, 1)
            partner = jnp.where((lane % (2 * quarter)) < quarter,
                                pltpu.roll(kk, HEAD_DIM - quarter, 1),
                                pltpu.roll(kk, quarter, 1))
            k_scr[rows, :] = (kk * cos_ref[rows, :] + partner * sin_ref[rows, :]).astype(BF16)
            return carry
        lax.fori_loop(0, nk, fill, 0)

    for g in range(GQA_GROUP):
        qt = q_ref[g].astype(F32).T
        partner = jnp.concatenate([qt[quarter:2 * quarter], qt[0:quarter],
                                   qt[3 * quarter:], qt[2 * quarter:3 * quarter]], axis=0)
        qt_scr[g] = (qt * cost_ref[...] + partner * sint_ref[...]).astype(BF16)
    m_scr[...] = jnp.full(m_scr.shape, -jnp.inf, F32)
    acc_scr[...] = jnp.zeros(acc_scr.shape, F32)

    def step(c_cur, buf_cur, c_next=None, buf_next=None):
        vt = vt_scr[c_cur]
        if c_next is not None:
            start = pl.multiple_of(c_next * GQA_TK, GQA_TK)
            k = k_scr[pl.ds(start, GQA_TK), :]
        for g in range(GQA_GROUP):
            if c_next is not None:
                buf_next[g] = jnp.dot(k, qt_scr[g], preferred_element_type=F32)
            s = buf_cur[g]
            m_old = m_scr[g]
            m_new = jnp.maximum(m_old, jnp.max(s, axis=0, keepdims=True))
            alpha = jnp.exp2(m_old - m_new)
            p = jnp.exp2(s - m_new).astype(BF16)
            acc_scr[g] = alpha * acc_scr[g] + jnp.dot(vt, p, preferred_element_type=F32)
            m_scr[g] = m_new

    scores(0, sa_scr)

    def pair(j, carry):
        step(2 * j, sa_scr, 2 * j + 1, sb_scr)
        step(2 * j + 1, sb_scr, 2 * j + 2, sa_scr)
        return carry

    lax.fori_loop(0, nk // 2 - 1, pair, 0)
    step(nk - 2, sa_scr, nk - 1, sb_scr)
    step(nk - 1, sb_scr)
    for g in range(GQA_GROUP):
        cols = slice(g * HEAD_DIM, (g + 1) * HEAD_DIM)
        acc = acc_scr[g]
        o = (acc[0:HEAD_DIM] / acc[HEAD_DIM:HEAD_DIM + 1]).T
        o_ref[:, cols] = (o * z_ref[g].astype(F32)).astype(o_ref.dtype)


def _gqa(proj, batch, seq_len, cos_t, sin_t, cos_tt, sin_tt):
    t = proj.shape[1]
    nk = seq_len // GQA_TK
    tq = GQA_TQ if nk > GQA_FEW_CHUNKS else 2 * GQA_TQ
    assert seq_len % tq == 0 and seq_len % (2 * GQA_TK) == 0
    nq = seq_len // tq
    gw = GQA_GROUP * HEAD_DIM
    return pl.pallas_call(
        functools.partial(_gqa_kernel, nk=seq_len // GQA_TK),
        name="gqa",
        grid=(batch, GQA_KV_HEADS, nq),
        in_specs=[
            pl.BlockSpec((GQA_GROUP, tq, HEAD_DIM), lambda b, g, i: (_slab(OFF_GQ_Q) // gw + g, b * nq + i, 0)),
            pl.BlockSpec((None, seq_len, HEAD_DIM), lambda b, g, i: (_slab(OFF_GQ_K) // HEAD_DIM + g, b, 0)),
            pl.BlockSpec((None, seq_len, HEAD_DIM), lambda b, g, i: (_slab(OFF_GQ_V) // HEAD_DIM + g, b, 0)),
            pl.BlockSpec((GQA_GROUP, tq, HEAD_DIM), lambda b, g, i: (_slab(OFF_GQ_Z) // gw + g, b * nq + i, 0)),
            pl.BlockSpec((seq_len, HEAD_DIM), lambda b, g, i: (0, 0), pipeline_mode=pl.Buffered(1)),
            pl.BlockSpec((seq_len, HEAD_DIM), lambda b, g, i: (0, 0), pipeline_mode=pl.Buffered(1)),
            pl.BlockSpec((HEAD_DIM, tq), lambda b, g, i: (0, i)),
            pl.BlockSpec((HEAD_DIM, tq), lambda b, g, i: (0, i)),
        ],
        out_specs=pl.BlockSpec((tq, gw), lambda b, g, i: (b * nq + i, g)),
        out_shape=jax.ShapeDtypeStruct((t, GQ_W), BF16),
        scratch_shapes=[
            pltpu.VMEM((GQA_GROUP, HEAD_DIM, tq), BF16),
            pltpu.VMEM((seq_len, HEAD_DIM), BF16),
            pltpu.VMEM((seq_len // GQA_TK, HEAD_DIM + GQA_ONES_ROWS, GQA_TK), BF16),
            pltpu.VMEM((GQA_GROUP, 1, tq), F32),
            pltpu.VMEM((GQA_GROUP, HEAD_DIM + GQA_ONES_ROWS, tq), F32),
            pltpu.VMEM((GQA_GROUP, GQA_TK, tq), F32),
            pltpu.VMEM((GQA_GROUP, GQA_TK, tq), F32),
        ],
        compiler_params=pltpu.CompilerParams(
            dimension_semantics=("parallel", "parallel", "arbitrary"),
            vmem_limit_bytes=VMEM_LIMIT_BYTES),
    )(proj, proj, proj, proj, cos_t, sin_t, cos_tt, sin_tt)


def _tail_kernel(x_ref, oa_ref, ob_ref, ga_ref, gb_ref, wa_ref, wb_ref, wo_ref, y_ref):
    oa = jnp.concatenate([oa_ref[h] for h in range(NA_HEADS)], axis=1)
    pa = jnp.dot(oa, wa_ref[...], preferred_element_type=F32)
    pb = jnp.dot(ob_ref[...], wb_ref[...], preferred_element_type=F32)
    merged = jnp.concatenate(
        [(ga_ref[c].astype(F32) * pa[:, c * HEAD_DIM:(c + 1) * HEAD_DIM]
          + gb_ref[c].astype(F32) * pb[:, c * HEAD_DIM:(c + 1) * HEAD_DIM]).astype(BF16)
         for c in range(D_MODEL // HEAD_DIM)], axis=1)
    y_ref[...] = x_ref[...] + jnp.dot(merged, wo_ref[...], preferred_element_type=F32)


def _tail(x2d, oa, ob, proj, wa_bf, wb_bf, wo_bf):
    t = x2d.shape[0]
    assert t % TAIL_TM == 0
    resident = dict(pipeline_mode=pl.Buffered(1))
    return pl.pallas_call(
        _tail_kernel,
        name="tail",
        grid=(t // TAIL_TM,),
        in_specs=[
            pl.BlockSpec((TAIL_TM, D_MODEL), lambda i: (i, 0)),
            pl.BlockSpec((NA_HEADS, TAIL_TM, HEAD_DIM), lambda i: (0, i, 0)),
            pl.BlockSpec((TAIL_TM, GQ_W), lambda i: (i, 0)),
            pl.BlockSpec((D_MODEL // HEAD_DIM, TAIL_TM, HEAD_DIM), lambda i: (_slab(OFF_GA) // D_MODEL, i, 0)),
            pl.BlockSpec((D_MODEL // HEAD_DIM, TAIL_TM, HEAD_DIM), lambda i: (_slab(OFF_GB) // D_MODEL, i, 0)),
            pl.BlockSpec((NA_W, D_MODEL), lambda i: (0, 0), **resident),
            pl.BlockSpec((GQ_W, D_MODEL), lambda i: (0, 0), **resident),
            pl.BlockSpec((D_MODEL, D_MODEL), lambda i: (0, 0), **resident),
        ],
        out_specs=pl.BlockSpec((TAIL_TM, D_MODEL), lambda i: (i, 0)),
        out_shape=jax.ShapeDtypeStruct((t, D_MODEL), F32),
        compiler_params=pltpu.CompilerParams(
            dimension_semantics=("parallel",),
            vmem_limit_bytes=VMEM_LIMIT_BYTES),
    )(x2d, oa, ob, proj, proj, wa_bf, wb_bf, wo_bf)


def _rope_tables(seq_len):
    pos = np.arange(seq_len)
    axis = HEAD_DIM // 2
    freqs = ROPE_THETA ** (-np.arange(0, axis, 2, dtype=np.float64) / axis)
    ang_r = (pos // GRID_W)[:, None] * freqs[None, :]
    ang_c = (pos % GRID_W)[:, None] * freqs[None, :]
    cos_t = np.concatenate([np.cos(ang_r), np.cos(ang_r), np.cos(ang_c), np.cos(ang_c)], axis=-1)
    sin_t = np.concatenate([-np.sin(ang_r), np.sin(ang_r), -np.sin(ang_c), np.sin(ang_c)], axis=-1)
    tables = (cos_t, sin_t, cos_t.T, sin_t.T)
    return tuple(jnp.asarray(np.ascontiguousarray(tbl), dtype=F32) for tbl in tables)


def _layer(x, norm_g, w_in, head_g, bias_tbl, wa_bf, wb_bf, gate_bias, wo_bf):
    batch, seq_len, _ = x.shape
    x2d = x.reshape(batch * seq_len, D_MODEL)
    proj = _proj(x2d, norm_g, w_in, head_g, gate_bias)
    oa = _na(proj, batch, seq_len, bias_tbl)
    ob = _gqa(proj, batch, seq_len, *_rope_tables(seq_len))
    y = _tail(x2d, oa, ob, proj, wa_bf, wb_bf, wo_bf)
    return y.reshape(batch, seq_len, D_MODEL)


def kernel(x_prompt, x_sample, norm_g, w_in, na_q_g, na_k_g, na_rpb, gq_q_g, gq_k_g,
           w_branch_a, w_branch_b, gate_bias, w_out):
    depth = norm_g.shape[0]
    y_prompt, y_sample = x_prompt, x_sample
    for l in range(depth):
        head_g = jnp.stack([na_q_g[l] * EXP2_SCALE, na_k_g[l], gq_q_g[l] * EXP2_SCALE, gq_k_g[l]]).astype(F32)
        params = (norm_g[l][None, :].astype(F32), w_in[l].astype(BF16), head_g,
                  _na_bias_table(na_rpb[l]), w_branch_a[l].astype(BF16), w_branch_b[l].astype(BF16),
                  gate_bias[l][None, :].astype(F32), w_out[l].astype(BF16))
        y_prompt = _layer(y_prompt, *params)
        y_sample = _layer(y_sample, *params)
    return (y_prompt, y_sample)
```
